```python
import jax, jax.numpy as jnp
from jax import lax
import numpy as np

D_MODEL = 2048
BATCH = 4
SEQ = 4096
DEPTH = 2

HEAD_DIM = 64
RWKV_WIDTH = 3 * D_MODEL // 8
POOL_WIDTH = D_MODEL // 4
SB_WIDTH = D_MODEL - RWKV_WIDTH - POOL_WIDTH
D_MIX = RWKV_WIDTH + POOL_WIDTH + SB_WIDTH
RWKV_HEADS = RWKV_WIDTH // HEAD_DIM
SB_HEADS = SB_WIDTH // HEAD_DIM
DECAY_RANK = 64
ICLR_RANK = 64
POOL_WINDOWS = (2, 4, 8, 16)
POOL_GROUPS = len(POOL_WINDOWS)
POOL_GROUP_WIDTH = POOL_WIDTH // POOL_GROUPS
SB_BLOCK = 128
RMS_EPS = 1e-6
GN_EPS = 64e-5
A_COLS = 4 * RWKV_WIDTH + DECAY_RANK + ICLR_RANK
B_COLS = 2 * POOL_WIDTH
C_COLS = 4 * SB_WIDTH
IN_COLS = A_COLS + B_COLS + C_COLS

kernel_name = "hymba_rwkv7_pool_stickbreak"


def _rms(x, g):
    xf = x.astype(jnp.float32)
    y = xf * lax.rsqrt(jnp.mean(jnp.square(xf), axis=-1, keepdims=True) + RMS_EPS)
    return (y * g.astype(jnp.float32)).astype(x.dtype)


def _split(u, sizes):
    cuts = [int(c) for c in np.cumsum(sizes)[:-1]]
    return jnp.split(u, cuts, axis=-1)


def _token_shift(u, mu):
    prev = jnp.pad(u, ((0, 0), (1, 0), (0, 0)))[:, :-1]
    return u + (prev - u) * mu


def _rwkv7(r, k, v, wd, ad, w_up, w0, a_up, a0, k_k, k_a, r_k, gn_g, gn_b):
    f32 = jnp.float32
    B, S, C = r.shape
    H, N = C // HEAD_DIM, HEAD_DIM
    r, k, v, wd, ad = (t.astype(f32) for t in (r, k, v, wd, ad))
    logw = -jax.nn.softplus(-(w0 + jnp.tanh(wd) @ w_up)) - 0.5
    decay = jnp.exp(-jnp.exp(logw))
    a = jax.nn.sigmoid(a0 + ad @ a_up)
    kk = (k * k_k).reshape(B, S, H, N)
    kk = kk / jnp.maximum(jnp.sqrt(jnp.sum(jnp.square(kk), axis=-1, keepdims=True)), 1e-12)
    k = k * (1.0 + (a - 1.0) * k_a)

    def heads_t(t):
        return t.reshape(B, S, H, N).transpose(1, 0, 2, 3)

    xs = (heads_t(r), heads_t(k), heads_t(v), heads_t(decay),
          kk.transpose(1, 0, 2, 3), heads_t(a))

    def step(state, inp):
        r_t, k_t, v_t, w_t, kk_t, a_t = inp
        sa = jnp.einsum('bhvk,bhk->bhv', state, -kk_t)
        state = (state * w_t[:, :, None, :]
                 + sa[..., None] * (kk_t * a_t)[:, :, None, :]
                 + v_t[..., None] * k_t[:, :, None, :])
        return state, jnp.einsum('bhvk,bhk->bhv', state, r_t)

    _, y = lax.scan(step, jnp.zeros((B, H, N, N), f32), xs)
    y = y.transpose(1, 0, 2, 3)
    mu = jnp.mean(y, axis=-1, keepdims=True)
    var = jnp.mean(jnp.square(y - mu), axis=-1, keepdims=True)
    y = ((y - mu) * lax.rsqrt(var + GN_EPS)).reshape(B, S, C) * gn_g + gn_b
    bonus = jnp.sum((r * k).reshape(B, S, H, N) * r_k, axis=-1, keepdims=True) * v.reshape(B, S, H, N)
    return y + bonus.reshape(B, S, C)


def _pool(u, pool_w, pool_scale):
    f32 = jnp.float32
    B, S, C = u.shape
    uf = u.astype(f32)
    cs = jnp.cumsum(uf, axis=1)
    pos = jnp.arange(1, S + 1, dtype=f32)[None, :, None]
    groups = []
    for gi, win in enumerate(POOL_WINDOWS):
        sl = slice(gi * POOL_GROUP_WIDTH, (gi + 1) * POOL_GROUP_WIDTH)
        c = cs[..., sl]
        lag = jnp.pad(c, ((0, 0), (win, 0), (0, 0)))[:, :S]
        mean = (c - lag) / jnp.minimum(pos, float(win))
        groups.append(mean - uf[..., sl])
    d = jnp.stack(groups, axis=2)
    y = jnp.einsum('bsgc,gcd->bsgd', d, pool_w.astype(f32)).reshape(B, S, C)
    return y * pool_scale


def _stick_breaking(q, k, v, qn_g, kn_g):
    f32 = jnp.float32
    B, S, C = q.shape
    H = C // HEAD_DIM

    def heads(t):
        return t.reshape(B, S, H, HEAD_DIM).transpose(0, 2, 1, 3)

    q = _rms(heads(q), qn_g).astype(f32) * (HEAD_DIM ** -0.5)
    k = _rms(heads(k), kn_g).astype(f32)
    v = heads(v).astype(f32)
    outs = []
    for q0 in range(0, S, SB_BLOCK):
        kend = q0 + SB_BLOCK
        z = jnp.einsum('bhqd,bhkd->bhqk', q[:, :, q0:kend], k[:, :, :kend])
        causal = jnp.arange(kend)[None, :] < (q0 + jnp.arange(SB_BLOCK))[:, None]
        log_1m = jnp.where(causal, jax.nn.log_sigmoid(-z), 0.0)
        after = lax.cumsum(log_1m, axis=3, reverse=True) - log_1m
        w = jnp.where(causal, jnp.exp(jax.nn.log_sigmoid(z) + after), 0.0)
        outs.append(jnp.einsum('bhqk,bhkd->bhqd', w, v[:, :, :kend]))
    o = jnp.concatenate(outs, axis=2)
    return o.transpose(0, 2, 1, 3).reshape(B, S, C)


def setup_inputs(seed: int = 0) -> dict:
    key = jax.random.key(seed)
    ks = jax.random.split(key, 20)
    f32 = jnp.float32
    nrm = lambda kk, shape, s: jax.random.normal(kk, shape, f32) * s
    L = DEPTH
    return {
        "x": nrm(ks[0], (BATCH, SEQ, D_MODEL), 1.0),
        "norm_g": 1.0 + nrm(ks[1], (L, D_MODEL), 0.02),
        "w_in": nrm(ks[2], (L, D_MODEL, IN_COLS), D_MODEL ** -0.5),
        "mu_a": jax.random.uniform(ks[3], (L, A_COLS), f32, 0.0, 1.0),
        "w_up": nrm(ks[4], (L, DECAY_RANK, RWKV_WIDTH), 0.1 * DECAY_RANK ** -0.5),
        "w0": jax.random.uniform(ks[5], (L, RWKV_WIDTH), f32, -6.0, 1.0),
        "a_up": nrm(ks[6], (L, ICLR_RANK, RWKV_WIDTH), 0.1 * ICLR_RANK ** -0.5),
        "a0": nrm(ks[7], (L, RWKV_WIDTH), 0.1),
        "k_k": 0.85 + nrm(ks[8], (L, RWKV_WIDTH), 0.02),
        "k_a": 1.0 + nrm(ks[9], (L, RWKV_WIDTH), 0.02),
        "r_k": nrm(ks[10], (L, RWKV_HEADS, HEAD_DIM), 0.1),
        "gn_g": 1.0 + nrm(ks[11], (L, RWKV_WIDTH), 0.02),
        "gn_b": nrm(ks[12], (L, RWKV_WIDTH), 0.02),
        "pool_w": nrm(ks[13], (L, POOL_GROUPS, POOL_GROUP_WIDTH, POOL_GROUP_WIDTH), POOL_GROUP_WIDTH ** -0.5),
        "pool_scale": 1.0 + nrm(ks[14], (L, POOL_WIDTH), 0.02),
        "qn_g": 1.0 + nrm(ks[15], (L, HEAD_DIM), 0.02),
        "kn_g": 1.0 + nrm(ks[16], (L, HEAD_DIM), 0.02),
        "w_out": nrm(ks[17], (L, D_MIX, D_MODEL), D_MIX ** -0.5),
    }


def reference(x, norm_g, w_in, mu_a, w_up, w0, a_up, a0, k_k, k_a, r_k, gn_g, gn_b,
              pool_w, pool_scale, qn_g, kn_g, w_out):
    f32 = jnp.float32
    for l in range(DEPTH):
        h = _rms(x, norm_g[l])
        proj = h @ w_in[l]
        pa, pb, pc = _split(proj, (A_COLS, B_COLS, C_COLS))
        pa = _token_shift(pa, mu_a[l])
        r, k, v, ga, wd, ad = _split(pa, (RWKV_WIDTH,) * 4 + (DECAY_RANK, ICLR_RANK))
        ya = _rwkv7(r, k, v, wd, ad, w_up[l], w0[l], a_up[l], a0[l], k_k[l], k_a[l],
                    r_k[l], gn_g[l], gn_b[l]) * jax.nn.silu(ga.astype(f32))
        pu, pg = _split(pb, (POOL_WIDTH, POOL_WIDTH))
        yb = _pool(pu, pool_w[l], pool_scale[l]) * jax.nn.silu(pg.astype(f32))
        qc, kc, vc, gc = _split(pc, (SB_WIDTH,) * 4)
        yc = _stick_breaking(qc, kc, vc, qn_g[l], kn_g[l]) * jax.nn.silu(gc.astype(f32))
        y = jnp.concatenate([ya, yb, yc], axis=-1).astype(x.dtype) @ w_out[l]
        x = x + y.astype(x.dtype)
    return x
```

```python
import functools

import jax
import jax.numpy as jnp
from jax import lax
from jax.experimental import pallas as pl
from jax.experimental.pallas import tpu as pltpu

F32 = jnp.float32
BF16 = jnp.bfloat16

D_MODEL = 2048
HEAD_DIM = 64
PAIR = 2 * HEAD_DIM
RWKV_WIDTH = 768
POOL_WIDTH = 512
SB_WIDTH = 768
LORA_RANK = 64
POOL_WINDOWS = (2, 4, 8, 16)
POOL_GROUP_WIDTH = POOL_WIDTH // len(POOL_WINDOWS)
POOL_HALO = 16
RMS_EPS = 1e-6
GN_EPS = 64e-5
A_COLS = 4 * RWKV_WIDTH + 2 * LORA_RANK
B_COLS = 2 * POOL_WIDTH
C_COLS = 4 * SB_WIDTH
IN_COLS = A_COLS + B_COLS + C_COLS
N_PAIRS = RWKV_WIDTH // PAIR

A_BLK = 0
LORA_BLK = 4 * RWKV_WIDTH // PAIR
B_BLK = A_COLS // PAIR
C_BLK = (A_COLS + B_COLS) // PAIR

VMEM_LIMIT = 56 * 1024 * 1024

NN = (((1,), (0,)), ((), ()))
NT = (((1,), (1,)), ((), ()))
TN = (((0,), (0,)), ((), ()))


def _dot32(a, b, dims=NN):
    return lax.dot_general(a, b, dims, precision=lax.Precision.HIGHEST, preferred_element_type=F32)


def _dot16(a, b, dims=NN):
    return lax.dot_general(a.astype(BF16), b.astype(BF16), dims, preferred_element_type=F32)


def _sigmoid(x):
    return 1.0 / (1.0 + jnp.exp(-x))


def _silu(x):
    return x * _sigmoid(x)


def _softplus(x):
    return jnp.maximum(x, 0.0) + jnp.log(1.0 + jnp.exp(-jnp.abs(x)))


def _head_block_ones():
    r = lax.broadcasted_iota(jnp.int32, (PAIR, PAIR), 0) // HEAD_DIM
    c = lax.broadcasted_iota(jnp.int32, (PAIR, PAIR), 1) // HEAD_DIM
    return (r == c).astype(F32)


def _params(sem):
    return pltpu.CompilerParams(dimension_semantics=sem, vmem_limit_bytes=VMEM_LIMIT)


def _in_proj_kernel(x_ref, g_ref, w_ref, o_ref, h_ref, *, row_chunk):
    @pl.when(pl.program_id(1) == 0)
    def _norm():
        def body(c, carry):
            rows = pl.ds(pl.multiple_of(c * row_chunk, row_chunk), row_chunk)
            x = x_ref[rows, :]
            ms = jnp.mean(x * x, axis=-1, keepdims=True)
            h_ref[rows, :] = (x * lax.rsqrt(ms + RMS_EPS) * g_ref[...]).astype(BF16)
            return carry
        lax.fori_loop(0, x_ref.shape[0] // row_chunk, body, 0)

    o_ref[...] = jnp.dot(h_ref[...], w_ref[...], preferred_element_type=F32)


def _in_proj(x2d, g, w_bf16, *, tm=512, tn=2432):
    t, d = x2d.shape
    n = w_bf16.shape[1]
    return pl.pallas_call(
        functools.partial(_in_proj_kernel, row_chunk=128),
        grid=(t // tm, n // tn),
        in_specs=[
            pl.BlockSpec((tm, d), lambda i, j: (i, 0)),
            pl.BlockSpec((1, d), lambda i, j: (0, 0)),
            pl.BlockSpec((d, tn), lambda i, j: (0, j)),
        ],
        out_specs=pl.BlockSpec((tm, tn), lambda i, j: (i, j)),
        out_shape=jax.ShapeDtypeStruct((t, n), F32),
        scratch_shapes=[pltpu.VMEM((tm, d), BF16)],
        compiler_params=_params(("parallel", "arbitrary")),
        name="in_proj",
    )(x2d, g, w_bf16)


def _out_proj_kernel(ya_ref, yb_ref, yc_ref, wa_ref, wb_ref, wc_ref, x_ref, o_ref):
    acc = jnp.dot(ya_ref[...], wa_ref[...], preferred_element_type=F32)
    acc += jnp.dot(yb_ref[...], wb_ref[...], preferred_element_type=F32)
    acc += jnp.dot(yc_ref[...], wc_ref[...], preferred_element_type=F32)
    o_ref[...] = x_ref[...] + acc


def _out_proj(ya, yb, yc, wa, wb, wc, x2d, *, tm=512):
    t, d = x2d.shape
    row = lambda i: (i, 0)
    whole = lambda i: (0, 0)
    return pl.pallas_call(
        _out_proj_kernel,
        grid=(t // tm,),
        in_specs=[
            pl.BlockSpec((tm, RWKV_WIDTH), row),
            pl.BlockSpec((tm, POOL_WIDTH), row),
            pl.BlockSpec((tm, SB_WIDTH), row),
            pl.BlockSpec((RWKV_WIDTH, d), whole),
            pl.BlockSpec((POOL_WIDTH, d), whole),
            pl.BlockSpec((SB_WIDTH, d), whole),
            pl.BlockSpec((tm, d), row),
        ],
        out_specs=pl.BlockSpec((tm, d), row),
        out_shape=jax.ShapeDtypeStruct((t, d), F32),
        compiler_params=_params(("parallel",)),
        name="out_proj",
    )(ya, yb, yc, wa, wb, wc, x2d)


def _pool_kernel(*refs, ts):
    n_g = len(POOL_WINDOWS)
    cur_refs = refs[0:n_g]
    halo_refs = refs[n_g:2 * n_g]
    gate_refs = refs[2 * n_g:3 * n_g]
    w_ref, scale_ref, o_ref, ext_ref = refs[3 * n_g:]
    i = pl.program_id(1)
    pos = (i * ts + 1 + lax.broadcasted_iota(jnp.int32, (ts, 1), 0)).astype(F32)
    for gi, win in enumerate(POOL_WINDOWS):
        cur = cur_refs[gi][0]
        ext_ref[0:POOL_HALO, :] = jnp.where(i == 0, 0.0, halo_refs[gi][0])
        ext_ref[POOL_HALO:, :] = cur
        acc = cur
        for lag in range(1, win):
            acc = acc + ext_ref[POOL_HALO - lag:POOL_HALO - lag + ts, :]
        d = acc / jnp.minimum(pos, float(win)) - cur
        y = _dot16(d, w_ref[gi])
        cols = slice(gi * POOL_GROUP_WIDTH, (gi + 1) * POOL_GROUP_WIDTH)
        y = y * scale_ref[:, cols] * _silu(gate_refs[gi][0])
        o_ref[0, :, cols] = y.astype(o_ref.dtype)


def _pool(proj, pool_w, pool_scale, *, ts=512):
    b, s, _ = proj.shape
    n_g = len(POOL_WINDOWS)
    halo_blocks = ts // POOL_HALO
    cur_specs = [pl.BlockSpec((1, ts, POOL_GROUP_WIDTH), lambda bi, i, g=g: (bi, i, B_BLK + g))
                 for g in range(n_g)]
    halo_specs = [pl.BlockSpec((1, POOL_HALO, POOL_GROUP_WIDTH),
                               lambda bi, i, g=g: (bi, jnp.maximum(i * halo_blocks - 1, 0), B_BLK + g))
                  for g in range(n_g)]
    gate_specs = [pl.BlockSpec((1, ts, POOL_GROUP_WIDTH), lambda bi, i, g=g: (bi, i, B_BLK + n_g + g))
                  for g in range(n_g)]
    return pl.pallas_call(
        functools.partial(_pool_kernel, ts=ts),
        grid=(b, s // ts),
        in_specs=cur_specs + halo_specs + gate_specs + [
            pl.BlockSpec((n_g, POOL_GROUP_WIDTH, POOL_GROUP_WIDTH), lambda bi, i: (0, 0, 0)),
            pl.BlockSpec((1, POOL_WIDTH), lambda bi, i: (0, 0)),
        ],
        out_specs=pl.BlockSpec((1, ts, POOL_WIDTH), lambda bi, i: (bi, i, 0)),
        out_shape=jax.ShapeDtypeStruct((b, s, POOL_WIDTH), BF16),
        scratch_shapes=[pltpu.VMEM((ts + POOL_HALO, POOL_GROUP_WIDTH), F32)],
        compiler_params=_params(("parallel", "arbitrary")),
        name="pool",
    )(*([proj] * (3 * n_g)), pool_w, pool_scale)


def _sb_kernel(q_ref, k_ref, v_ref, g_ref, qg_ref, kg_ref, o_ref, kn_ref, vb_ref, *, blk):
    i = pl.program_id(2)
    seq = k_ref.shape[1]
    ones_h = _head_block_ones()
    head0 = lax.broadcasted_iota(jnp.int32, (1, PAIR), 1) < HEAD_DIM

    @pl.when(i == 0)
    def _prep_keys():
        def body(c, carry):
            rows = pl.ds(pl.multiple_of(c * blk, blk), blk)
            k = k_ref[0, rows, :]
            ms = _dot32(k * k, ones_h) * (1.0 / HEAD_DIM)
            kn_ref[rows, :] = (k * lax.rsqrt(ms + RMS_EPS) * kg_ref[...]).astype(BF16)
            vb_ref[rows, :] = v_ref[0, rows, :].astype(BF16)
            return carry
        lax.fori_loop(0, seq // blk, body, 0)

    q = q_ref[0]
    ms = _dot32(q * q, ones_h) * (1.0 / HEAD_DIM)
    qn = q * lax.rsqrt(ms + RMS_EPS) * qg_ref[...] * (HEAD_DIM ** -0.5)
    qq = jnp.concatenate([jnp.where(head0, qn, 0.0), jnp.where(head0, 0.0, qn)], axis=0).astype(BF16)

    r_i = lax.broadcasted_iota(jnp.int32, (blk, blk), 0)
    c_i = lax.broadcasted_iota(jnp.int32, (blk, blk), 1)
    suffix = (r_i >= c_i).astype(BF16)
    r_2 = lax.broadcasted_iota(jnp.int32, (2 * blk, blk), 0) % blk
    causal = lax.broadcasted_iota(jnp.int32, (2 * blk, blk), 1) < r_2

    def block(j, carry, acc, diagonal):
        rows = pl.ds(pl.multiple_of(j * blk, blk), blk)
        kb = kn_ref[rows, :]
        vb = vb_ref[rows, :]
        z = lax.dot_general(qq, kb, NT, preferred_element_type=F32)
        log_1m = -_softplus(z)
        if diagonal:
            log_1m = jnp.where(causal, log_1m, 0.0)
        hi = log_1m.astype(BF16)
        lo = (log_1m - hi.astype(F32)).astype(BF16)
        csum = (jnp.dot(hi, suffix, preferred_element_type=F32)
                + jnp.dot(lo, suffix, preferred_element_type=F32) + carry)
        w = jnp.exp(z + csum)
        if diagonal:
            w = jnp.where(causal, w, 0.0)
        wb = w.astype(BF16)
        pv0 = jnp.dot(wb[:blk], vb, preferred_element_type=F32)
        pv1 = jnp.dot(wb[blk:], vb, preferred_element_type=F32)
        return csum[:, 0:1], acc + jnp.where(head0, pv0, pv1)

    carry, acc = block(i, jnp.zeros((2 * blk, 1), F32), jnp.zeros((blk, PAIR), F32), True)

    def body(n, state):
        return block(i - 1 - n, state[0], state[1], False)

    carry, acc = lax.fori_loop(0, i, body, (carry, acc))
    o_ref[0] = (acc * _silu(g_ref[0])).astype(o_ref.dtype)


def _stick_breaking(proj, qn_g, kn_g, *, blk=256):
    b, s, _ = proj.shape
    qg = jnp.tile(qn_g.reshape(1, HEAD_DIM), (1, 2))
    kg = jnp.tile(kn_g.reshape(1, HEAD_DIM), (1, 2))
    vec = pl.BlockSpec((1, PAIR), lambda bi, p, i: (0, 0))
    return pl.pallas_call(
        functools.partial(_sb_kernel, blk=blk),
        grid=(b, N_PAIRS, s // blk),
        in_specs=[
            pl.BlockSpec((1, blk, PAIR), lambda bi, p, i: (bi, i, C_BLK + p)),
            pl.BlockSpec((1, s, PAIR), lambda bi, p, i: (bi, 0, C_BLK + N_PAIRS + p)),
            pl.BlockSpec((1, s, PAIR), lambda bi, p, i: (bi, 0, C_BLK + 2 * N_PAIRS + p)),
            pl.BlockSpec((1, blk, PAIR), lambda bi, p, i: (bi, i, C_BLK + 3 * N_PAIRS + p)),
            vec, vec,
        ],
        out_specs=pl.BlockSpec((1, blk, PAIR), lambda bi, p, i: (bi, i, p)),
        out_shape=jax.ShapeDtypeStruct((b, s, SB_WIDTH), BF16),
        scratch_shapes=[pltpu.VMEM((s, PAIR), BF16), pltpu.VMEM((s, PAIR), BF16)],
        compiler_params=_params(("parallel", "parallel", "arbitrary")),
        name="stick_breaking",
    )(proj, proj, proj, proj, qg, kg)


INV_BLOCK = 16


def _unit_lower_inverse(low):
    c = low.shape[0]
    r_i = lax.broadcasted_iota(jnp.int32, (c, c), 0)
    c_i = lax.broadcasted_iota(jnp.int32, (c, c), 1)
    eye = (r_i == c_i).astype(F32)
    same_block = (r_i // INV_BLOCK) == (c_i // INV_BLOCK)
    diag = jnp.where(same_block, low, 0.0)
    off = jnp.where(same_block, 0.0, low)

    def neumann(n, order):
        inv = eye + n
        power = n
        span = 2
        while span < order:
            power = _dot32(power, power)
            inv = inv + _dot32(inv, power)
            span *= 2
        return inv

    inv_diag = neumann(diag, INV_BLOCK)
    inv_off = neumann(_dot32(inv_diag, off), c // INV_BLOCK)
    return _dot32(inv_off, inv_diag)


def _rwkv_kernel(r_ref, k_ref, v_ref, g_ref, lora_ref,
                 mu_r_ref, mu_k_ref, mu_v_ref, mu_g_ref, mu_l_ref,
                 w_up_ref, a_up_ref, w0_ref, a0_ref, kk_ref, ka_ref, rk_ref, gng_ref, gnb_ref,
                 o_ref, state_ref, prev_ref):
    ci = pl.program_id(2)
    c = r_ref.shape[1]

    @pl.when(ci == 0)
    def _reset():
        state_ref[...] = jnp.zeros_like(state_ref)
        prev_ref[...] = jnp.zeros_like(prev_ref)

    row = lax.broadcasted_iota(jnp.int32, (c, 1), 0)
    head0 = lax.broadcasted_iota(jnp.int32, (1, PAIR), 1) < HEAD_DIM
    ones_h = _head_block_ones()

    def shifted(ref, mu_ref, slot):
        x = ref[0]
        prev = jnp.where(row == 0, prev_ref[slot:slot + 1, :], pltpu.roll(x, 1, 0))
        prev_ref[slot:slot + 1, :] = x[c - 1:c, :]
        return x + (prev - x) * mu_ref[...]

    r = shifted(r_ref, mu_r_ref, 0)
    k = shifted(k_ref, mu_k_ref, 1)
    v = shifted(v_ref, mu_v_ref, 2)
    gate = shifted(g_ref, mu_g_ref, 3)
    lora = shifted(lora_ref, mu_l_ref, 4)

    log_w = -_softplus(-(w0_ref[...] + _dot32(jnp.tanh(lora), w_up_ref[...]))) - 0.5
    log_decay = -jnp.exp(log_w)
    iclr = _sigmoid(a0_ref[...] + _dot32(lora, a_up_ref[...]))
    kk = k * kk_ref[...]
    kk = kk / jnp.maximum(jnp.sqrt(_dot32(kk * kk, ones_h)), 1e-12)
    k = k * (1.0 + (iclr - 1.0) * ka_ref[...])

    r_i = lax.broadcasted_iota(jnp.int32, (c, c), 0)
    c_i = lax.broadcasted_iota(jnp.int32, (c, c), 1)
    lower = r_i >= c_i
    strict = r_i > c_i
    lower_2 = (lax.broadcasted_iota(jnp.int32, (c, 2 * c), 0)
               >= lax.broadcasted_iota(jnp.int32, (c, 2 * c), 1) % c)
    cum = _dot32(lower.astype(F32), log_decay)
    total = cum[c - 1:c, :]
    w_incl = jnp.exp(cum)
    w_excl = jnp.exp(cum - log_decay)
    w_inv = jnp.exp(-cum)
    w_rest = jnp.exp(total - cum)
    w_all = jnp.exp(total)

    a_t = -kk * w_excl
    r_t = r * w_incl
    b_t = kk * iclr * w_inv
    k_t = k * w_inv
    b_w = kk * iclr * w_rest
    k_w = k * w_rest

    lhs = jnp.concatenate([jnp.where(head0, a_t, 0.0), jnp.where(head0, 0.0, a_t),
                           jnp.where(head0, r_t, 0.0), jnp.where(head0, 0.0, r_t)], axis=0)
    rhs = jnp.concatenate([b_t, k_t], axis=0)
    m = _dot32(lhs, rhs, NT)

    state = state_ref[...]
    p = _dot32(a_t, state, NT)
    x0 = p + jnp.where(head0,
                       _dot32(jnp.where(strict, m[0:c, c:], 0.0), v),
                       _dot32(jnp.where(strict, m[c:2 * c, c:], 0.0), v))
    inv0 = _unit_lower_inverse(jnp.where(strict, m[0:c, :c], 0.0))
    inv1 = _unit_lower_inverse(jnp.where(strict, m[c:2 * c, :c], 0.0))
    u = jnp.where(head0, _dot32(inv0, x0), _dot32(inv1, x0))
    uv = jnp.concatenate([u, v], axis=0)
    y = _dot32(r_t, state, NT) + jnp.where(
        head0,
        _dot32(jnp.where(lower_2, m[2 * c:3 * c, :], 0.0), uv),
        _dot32(jnp.where(lower_2, m[3 * c:, :], 0.0), uv))
    state_ref[...] = state * w_all + ones_h * _dot32(uv, jnp.concatenate([b_w, k_w], axis=0), TN)

    mean = _dot32(y, ones_h) * (1.0 / HEAD_DIM)
    yc = y - mean
    var = _dot32(yc * yc, ones_h) * (1.0 / HEAD_DIM)
    yn = yc * lax.rsqrt(var + GN_EPS) * gng_ref[...] + gnb_ref[...]
    bonus = _dot32(r * k * rk_ref[...], ones_h) * v
    o_ref[0] = ((yn + bonus) * _silu(gate)).astype(o_ref.dtype)


def _rwkv(proj, mu_a, w_up, w0, a_up, a0, k_k, k_a, r_k, gn_g, gn_b, *, chunk=128):
    b, s, _ = proj.shape
    zeros = jnp.zeros((LORA_RANK, RWKV_WIDTH), F32)
    w_up_pad = jnp.concatenate([w_up, zeros], axis=0)
    a_up_pad = jnp.concatenate([zeros, a_up], axis=0)
    mu = mu_a.reshape(1, A_COLS)
    row = lambda x: x.reshape(1, RWKV_WIDTH)

    def act(blk_off):
        return pl.BlockSpec((1, chunk, PAIR), lambda bi, p, ci: (bi, ci, blk_off + p))

    def mu_spec(blk_off):
        return pl.BlockSpec((1, PAIR), lambda bi, p, ci: (0, blk_off + p))

    lora_spec = pl.BlockSpec((1, chunk, PAIR), lambda bi, p, ci: (bi, ci, LORA_BLK))
    mu_lora_spec = pl.BlockSpec((1, PAIR), lambda bi, p, ci: (0, LORA_BLK))
    up_spec = pl.BlockSpec((PAIR, PAIR), lambda bi, p, ci: (0, p))
    vec = pl.BlockSpec((1, PAIR), lambda bi, p, ci: (0, p))
    return pl.pallas_call(
        _rwkv_kernel,
        grid=(b, N_PAIRS, s // chunk),
        in_specs=[act(0), act(N_PAIRS), act(2 * N_PAIRS), act(3 * N_PAIRS), lora_spec,
                  mu_spec(0), mu_spec(N_PAIRS), mu_spec(2 * N_PAIRS), mu_spec(3 * N_PAIRS), mu_lora_spec,
                  up_spec, up_spec] + [vec] * 7,
        out_specs=pl.BlockSpec((1, chunk, PAIR), lambda bi, p, ci: (bi, ci, p)),
        out_shape=jax.ShapeDtypeStruct((b, s, RWKV_WIDTH), BF16),
        scratch_shapes=[pltpu.VMEM((PAIR, PAIR), F32), pltpu.VMEM((8, PAIR), F32)],
        compiler_params=_params(("parallel", "parallel", "arbitrary")),
        name="rwkv7",
    )(proj, proj, proj, proj, proj, mu, mu, mu, mu, mu, w_up_pad, a_up_pad,
      row(w0), row(a0), row(k_k), row(k_a), row(r_k), row(gn_g), row(gn_b))


def kernel(x, norm_g, w_in, mu_a, w_up, w0, a_up, a0, k_k, k_a, r_k, gn_g, gn_b, pool_w, pool_scale,
           qn_g, kn_g, w_out):
    b, s, d = x.shape
    depth = w_in.shape[0]
    x2d = x.reshape(b * s, d)
    for l in range(depth):
        proj = _in_proj(x2d, norm_g[l].reshape(1, d), w_in[l].astype(BF16)).reshape(b, s, IN_COLS)
        ya = _rwkv(proj, mu_a[l], w_up[l], w0[l], a_up[l], a0[l], k_k[l], k_a[l], r_k[l], gn_g[l], gn_b[l])
        yb = _pool(proj, pool_w[l], pool_scale[l].reshape(1, POOL_WIDTH))
        yc = _stick_breaking(proj, qn_g[l], kn_g[l])
        w_o = w_out[l].astype(BF16)
        x2d = _out_proj(ya.reshape(b * s, RWKV_WIDTH), yb.reshape(b * s, POOL_WIDTH),
                        yc.reshape(b * s, SB_WIDTH),
                        w_o[:RWKV_WIDTH], w_o[RWKV_WIDTH:RWKV_WIDTH + POOL_WIDTH],
                        w_o[RWKV_WIDTH + POOL_WIDTH:], x2d)
    return x2d.reshape(b, s, d)
```

```python
import functools

import jax
import jax.numpy as jnp
from jax import lax
from jax.experimental import pallas as pl
from jax.experimental.pallas import tpu as pltpu

F32 = jnp.float32
BF16 = jnp.bfloat16

D_MODEL = 2048
HEAD_DIM = 64
PAIR = 2 * HEAD_DIM
RWKV_WIDTH = 768
POOL_WIDTH = 512
SB_WIDTH = 768
LORA_RANK = 64
POOL_WINDOWS = (2, 4, 8, 16)
POOL_GROUP_WIDTH = POOL_WIDTH // len(POOL_WINDOWS)
POOL_HALO = 16
RMS_EPS = 1e-6
GN_EPS = 64e-5
SB_ROW_GROUP = 128
SB_UNDERFLOW_LOG2 = 127.0
LOG2_E = 1.4426950408889634
A_COLS = 4 * RWKV_WIDTH + 2 * LORA_RANK
B_COLS = 2 * POOL_WIDTH
C_COLS = 4 * SB_WIDTH
IN_COLS = A_COLS + B_COLS + C_COLS
N_PAIRS = RWKV_WIDTH // PAIR

A_BLK = 0
LORA_BLK = 4 * RWKV_WIDTH // PAIR
B_BLK = A_COLS // PAIR
C_BLK = (A_COLS + B_COLS) // PAIR

VMEM_LIMIT = 56 * 1024 * 1024

NN = (((1,), (0,)), ((), ()))
NT = (((1,), (1,)), ((), ()))
TN = (((0,), (0,)), ((), ()))


def _dot32(a, b, dims=NN):
    return lax.dot_general(a, b, dims, precision=lax.Precision.HIGHEST, preferred_element_type=F32)


def _dot16(a, b, dims=NN):
    return lax.dot_general(a.astype(BF16), b.astype(BF16), dims, preferred_element_type=F32)


def _bf16_terms(x, n):
    terms = []
    for _ in range(n - 1):
        t = x.astype(BF16)
        terms.append(t)
        x = x - t.astype(F32)
    terms.append(x.astype(BF16))
    return terms


def _mm(a, b, dims=NN, na=1, nb=1):
    a_terms = a if isinstance(a, list) else _bf16_terms(a, na)
    b_terms = b if isinstance(b, list) else _bf16_terms(b, nb)
    keep = max(len(a_terms), len(b_terms))
    out = None
    for i, at in enumerate(a_terms):
        for j, bt in enumerate(b_terms):
            if i + j < keep:
                term = lax.dot_general(at, bt, dims, preferred_element_type=F32)
                out = term if out is None else out + term
    return out


def _sigmoid(x):
    return 1.0 / (1.0 + jnp.exp(-x))


def _silu(x):
    return x * _sigmoid(x)


def _softplus(x):
    return jnp.maximum(x, 0.0) + jnp.log(1.0 + jnp.exp(-jnp.abs(x)))


def _head_block_ones():
    r = lax.broadcasted_iota(jnp.int32, (PAIR, PAIR), 0) // HEAD_DIM
    c = lax.broadcasted_iota(jnp.int32, (PAIR, PAIR), 1) // HEAD_DIM
    return (r == c).astype(F32)


def _params(sem):
    return pltpu.CompilerParams(dimension_semantics=sem, vmem_limit_bytes=VMEM_LIMIT)


def _in_proj_kernel(x_ref, g_ref, w_ref, o_ref, h_ref, *, row_chunk):
    @pl.when(pl.program_id(1) == 0)
    def _norm():
        def body(c, carry):
            rows = pl.ds(pl.multiple_of(c * row_chunk, row_chunk), row_chunk)
            x = x_ref[rows, :]
            ms = jnp.mean(x * x, axis=-1, keepdims=True)
            h_ref[rows, :] = (x * lax.rsqrt(ms + RMS_EPS) * g_ref[...]).astype(BF16)
            return carry
        lax.fori_loop(0, x_ref.shape[0] // row_chunk, body, 0)

    o_ref[...] = jnp.dot(h_ref[...], w_ref[...], preferred_element_type=F32)


def _in_proj(x2d, g, w_bf16, *, tm=512, tn=2432):
    t, d = x2d.shape
    n = w_bf16.shape[1]
    return pl.pallas_call(
        functools.partial(_in_proj_kernel, row_chunk=128),
        grid=(t // tm, n // tn),
        in_specs=[
            pl.BlockSpec((tm, d), lambda i, j: (i, 0)),
            pl.BlockSpec((1, d), lambda i, j: (0, 0)),
            pl.BlockSpec((d, tn), lambda i, j: (0, j)),
        ],
        out_specs=pl.BlockSpec((tm, tn), lambda i, j: (i, j)),
        out_shape=jax.ShapeDtypeStruct((t, n), F32),
        scratch_shapes=[pltpu.VMEM((tm, d), BF16)],
        compiler_params=_params(("parallel", "arbitrary")),
        name="in_proj",
    )(x2d, g, w_bf16)


def _out_proj_kernel(ya_ref, yb_ref, yc_ref, wa_ref, wb_ref, wc_ref, x_ref, o_ref):
    acc = jnp.dot(ya_ref[...], wa_ref[...], preferred_element_type=F32)
    acc += jnp.dot(yb_ref[...], wb_ref[...], preferred_element_type=F32)
    acc += jnp.dot(yc_ref[...], wc_ref[...], preferred_element_type=F32)
    o_ref[...] = x_ref[...] + acc


def _out_proj(ya, yb, yc, wa, wb, wc, x2d, *, tm=512):
    t, d = x2d.shape
    row = lambda i: (i, 0)
    whole = lambda i: (0, 0)
    return pl.pallas_call(
        _out_proj_kernel,
        grid=(t // tm,),
        in_specs=[
            pl.BlockSpec((tm, RWKV_WIDTH), row),
            pl.BlockSpec((tm, POOL_WIDTH), row),
            pl.BlockSpec((tm, SB_WIDTH), row),
            pl.BlockSpec((RWKV_WIDTH, d), whole),
            pl.BlockSpec((POOL_WIDTH, d), whole),
            pl.BlockSpec((SB_WIDTH, d), whole),
            pl.BlockSpec((tm, d), row),
        ],
        out_specs=pl.BlockSpec((tm, d), row),
        out_shape=jax.ShapeDtypeStruct((t, d), F32),
        compiler_params=_params(("parallel",)),
        name="out_proj",
    )(ya, yb, yc, wa, wb, wc, x2d)


def _pool_kernel(*refs, ts):
    n_g = len(POOL_WINDOWS)
    cur_refs = refs[0:n_g]
    halo_refs = refs[n_g:2 * n_g]
    gate_refs = refs[2 * n_g:3 * n_g]
    w_ref, scale_ref, o_ref, ext_ref = refs[3 * n_g:]
    i = pl.program_id(1)
    pos = (i * ts + 1 + lax.broadcasted_iota(jnp.int32, (ts, 1), 0)).astype(F32)
    for gi, win in enumerate(POOL_WINDOWS):
        cur = cur_refs[gi][0]
        ext_ref[0:POOL_HALO, :] = jnp.where(i == 0, 0.0, halo_refs[gi][0])
        ext_ref[POOL_HALO:, :] = cur
        acc = cur
        for lag in range(1, win):
            acc = acc + ext_ref[POOL_HALO - lag:POOL_HALO - lag + ts, :]
        d = acc / jnp.minimum(pos, float(win)) - cur
        y = _dot16(d, w_ref[gi])
        cols = slice(gi * POOL_GROUP_WIDTH, (gi + 1) * POOL_GROUP_WIDTH)
        y = y * scale_ref[:, cols] * _silu(gate_refs[gi][0])
        o_ref[0, :, cols] = y.astype(o_ref.dtype)


def _pool(proj, pool_w, pool_scale, *, ts=512):
    b, s, _ = proj.shape
    n_g = len(POOL_WINDOWS)
    halo_blocks = ts // POOL_HALO
    cur_specs = [pl.BlockSpec((1, ts, POOL_GROUP_WIDTH), lambda bi, i, g=g: (bi, i, B_BLK + g))
                 for g in range(n_g)]
    halo_specs = [pl.BlockSpec((1, POOL_HALO, POOL_GROUP_WIDTH),
                               lambda bi, i, g=g: (bi, jnp.maximum(i * halo_blocks - 1, 0), B_BLK + g))
                  for g in range(n_g)]
    gate_specs = [pl.BlockSpec((1, ts, POOL_GROUP_WIDTH), lambda bi, i, g=g: (bi, i, B_BLK + n_g + g))
                  for g in range(n_g)]
    return pl.pallas_call(
        functools.partial(_pool_kernel, ts=ts),
        grid=(b, s // ts),
        in_specs=cur_specs + halo_specs + gate_specs + [
            pl.BlockSpec((n_g, POOL_GROUP_WIDTH, POOL_GROUP_WIDTH), lambda bi, i: (0, 0, 0)),
            pl.BlockSpec((1, POOL_WIDTH), lambda bi, i: (0, 0)),
        ],
        out_specs=pl.BlockSpec((1, ts, POOL_WIDTH), lambda bi, i: (bi, i, 0)),
        out_shape=jax.ShapeDtypeStruct((b, s, POOL_WIDTH), BF16),
        scratch_shapes=[pltpu.VMEM((ts + POOL_HALO, POOL_GROUP_WIDTH), F32)],
        compiler_params=_params(("parallel", "arbitrary")),
        name="pool",
    )(*([proj] * (3 * n_g)), pool_w, pool_scale)


def _sb_kernel(q_ref, k_ref, v_ref, g_ref, qg_ref, kg_ref, o_ref, kn_ref, vb_ref, *, blk):
    i = pl.program_id(2)
    seq = k_ref.shape[1]
    ones_h = _head_block_ones()
    head0 = lax.broadcasted_iota(jnp.int32, (1, PAIR), 1) < HEAD_DIM

    @pl.when(i == 0)
    def _prep_keys():
        def body(c, carry):
            rows = pl.ds(pl.multiple_of(c * blk, blk), blk)
            k = k_ref[0, rows, :]
            ms = _dot32(k * k, ones_h) * (1.0 / HEAD_DIM)
            kn_ref[rows, :] = (k * lax.rsqrt(ms + RMS_EPS) * kg_ref[...]).astype(BF16)
            vb_ref[rows, :] = v_ref[0, rows, :].astype(BF16)
            return carry
        lax.fori_loop(0, seq // blk, body, 0)

    q = q_ref[0]
    ms = _dot32(q * q, ones_h) * (1.0 / HEAD_DIM)
    qn = q * lax.rsqrt(ms + RMS_EPS) * qg_ref[...] * (HEAD_DIM ** -0.5 * LOG2_E)
    qq = jnp.concatenate([jnp.where(head0, qn, 0.0), jnp.where(head0, 0.0, qn)], axis=0).astype(BF16)

    r_i = lax.broadcasted_iota(jnp.int32, (2 * blk, blk), 0) % blk
    c_i = lax.broadcasted_iota(jnp.int32, (2 * blk, blk), 1)
    suffix2 = (r_i >= c_i).astype(BF16)
    causal = c_i < r_i

    n_groups = 2 * blk // SB_ROW_GROUP

    def group(x, g):
        return x[g * SB_ROW_GROUP:(g + 1) * SB_ROW_GROUP]

    def walk(blocks, carry, acc):
        groups = range(n_groups)
        tiles = [(b, g) for b in range(len(blocks)) for g in groups]
        key_rows = [pl.ds(pl.multiple_of(j * blk, blk), blk) for j, _, _ in blocks]
        kbs = [kn_ref[rows, :] for rows in key_rows]
        vbs = [vb_ref[rows, :] for rows in key_rows]
        zs = {(b, g): lax.dot_general(group(qq, g), kbs[b], NT, preferred_element_type=F32) for b, g in tiles}
        drops = {t: jnp.maximum(z, 0.0) + jnp.log2(1.0 + jnp.exp2(-jnp.abs(z))) for t, z in zs.items()}
        for b, g in tiles:
            if blocks[b][1]:
                drops[b, g] = jnp.where(group(causal, g), drops[b, g], 0.0)
        his = {t: x.astype(BF16) for t, x in drops.items()}
        los = {t: (drops[t] - his[t].astype(F32)).astype(BF16) for t in tiles}
        sums = {t: jnp.dot(jnp.concatenate([his[t], los[t]], axis=1), suffix2, preferred_element_type=F32)
                for t in tiles}
        for b, (_, on_diagonal, valid) in enumerate(blocks):
            csums = [sums[b, g] + group(carry, g) for g in groups]
            ws = [jnp.exp2(zs[b, g] - csums[g]) for g in groups]
            if on_diagonal:
                ws = [jnp.where(group(causal, g), ws[g], 0.0) for g in groups]
            if valid is not None:
                ws = [jnp.where(valid, w, 0.0) for w in ws]
            pvs = [jnp.dot(w.astype(BF16), vbs[b], preferred_element_type=F32) for w in ws]
            pv0 = jnp.concatenate(pvs[:n_groups // 2], axis=0)
            pv1 = jnp.concatenate(pvs[n_groups // 2:], axis=0)
            acc = acc + jnp.where(head0, pv0, pv1)
            carry = jnp.concatenate([c[:, 0:1] for c in csums], axis=0)
        return carry, acc

    carry, acc = walk([(i, True, None), (jnp.maximum(i - 1, 0), False, i > 0)],
                      jnp.zeros((2 * blk, 1), F32), jnp.zeros((blk, PAIR), F32))

    def more(state):
        n, live = state[0], state[1]
        return jnp.logical_and(n < i, live)

    def body(state):
        n, _, carry, acc = state
        carry, acc = walk([(i - 1 - n, False, None)], carry, acc)
        return n + 1, jnp.min(carry) < SB_UNDERFLOW_LOG2, carry, acc

    _, _, carry, acc = lax.while_loop(more, body, (jnp.int32(1), jnp.min(carry) < SB_UNDERFLOW_LOG2, carry, acc))
    o_ref[0] = (acc * _silu(g_ref[0])).astype(o_ref.dtype)


def _stick_breaking(proj, qn_g, kn_g, *, blk=256):
    b, s, _ = proj.shape
    qg = jnp.tile(qn_g.reshape(1, HEAD_DIM), (1, 2))
    kg = jnp.tile(kn_g.reshape(1, HEAD_DIM), (1, 2))
    vec = pl.BlockSpec((1, PAIR), lambda bi, p, i: (0, 0))
    return pl.pallas_call(
        functools.partial(_sb_kernel, blk=blk),
        grid=(b, N_PAIRS, s // blk),
        in_specs=[
            pl.BlockSpec((1, blk, PAIR), lambda bi, p, i: (bi, i, C_BLK + p)),
            pl.BlockSpec((1, s, PAIR), lambda bi, p, i: (bi, 0, C_BLK + N_PAIRS + p)),
            pl.BlockSpec((1, s, PAIR), lambda bi, p, i: (bi, 0, C_BLK + 2 * N_PAIRS + p)),
            pl.BlockSpec((1, blk, PAIR), lambda bi, p, i: (bi, i, C_BLK + 3 * N_PAIRS + p)),
            vec, vec,
        ],
        out_specs=pl.BlockSpec((1, blk, PAIR), lambda bi, p, i: (bi, i, p)),
        out_shape=jax.ShapeDtypeStruct((b, s, SB_WIDTH), BF16),
        scratch_shapes=[pltpu.VMEM((s, PAIR), BF16), pltpu.VMEM((s, PAIR), BF16)],
        compiler_params=_params(("parallel", "parallel", "arbitrary")),
        name="stick_breaking",
    )(proj, proj, proj, proj, qg, kg)


INV_BLOCK = 16

RWKV_TERMS = {
    "lora": 2,
    "exact": 3,
    "scores": 1,
    "inverse": 1,
    "apply": 1,
}


def _unit_lower_inverses(lows):
    c = lows[0].shape[0]
    r_i = lax.broadcasted_iota(jnp.int32, (c, c), 0)
    c_i = lax.broadcasted_iota(jnp.int32, (c, c), 1)
    eye = (r_i == c_i).astype(F32)
    same_block = (r_i // INV_BLOCK) == (c_i // INV_BLOCK)
    diags = [jnp.where(same_block, low, 0.0) for low in lows]
    offs = [jnp.where(same_block, 0.0, low) for low in lows]

    n_inv = RWKV_TERMS["inverse"]

    def neumann(ns, order):
        invs = [eye + n for n in ns]
        powers = [_bf16_terms(n, n_inv) for n in ns]
        span = 2
        while span < order:
            powers = [_bf16_terms(_mm(pw, pw), n_inv) for pw in powers]
            invs = [inv + _mm(inv, pw, na=n_inv) for inv, pw in zip(invs, powers)]
            span *= 2
        return invs

    inv_diags = [_bf16_terms(inv, n_inv) for inv in neumann(diags, INV_BLOCK)]
    inv_offs = neumann([_mm(inv, off, nb=n_inv) for inv, off in zip(inv_diags, offs)], c // INV_BLOCK)
    return [_mm(inv_off, inv_diag, na=n_inv) for inv_off, inv_diag in zip(inv_offs, inv_diags)]


def _rwkv_kernel(r_ref, k_ref, v_ref, g_ref, lora_ref,
                 mu_r_ref, mu_k_ref, mu_v_ref, mu_g_ref, mu_l_ref,
                 w_up_ref, a_up_ref, w0_ref, a0_ref, kk_ref, ka_ref, rk_ref, gng_ref, gnb_ref,
                 o_ref, state_ref, prev_ref):
    ci = pl.program_id(2)
    c = r_ref.shape[1]
    n_pairs = r_ref.shape[2] // PAIR
    n_lora, n_exact, n_scores, n_apply = (RWKV_TERMS[s] for s in ("lora", "exact", "scores", "apply"))

    @pl.when(ci == 0)
    def _reset():
        state_ref[...] = jnp.zeros_like(state_ref)
        prev_ref[...] = jnp.zeros_like(prev_ref)

    row = lax.broadcasted_iota(jnp.int32, (c, 1), 0)
    head0 = lax.broadcasted_iota(jnp.int32, (1, PAIR), 1) < HEAD_DIM
    ones_h = _head_block_ones()
    ones_h16 = ones_h.astype(BF16)

    def pair(x, p):
        return x[:, p * PAIR:(p + 1) * PAIR]

    def head_sums(x):
        return jnp.concatenate([_mm(pair(x, p), ones_h16, na=n_exact) for p in range(n_pairs)], axis=1)

    def shifted(ref, mu_ref, slot):
        x = ref[0]
        width = x.shape[1]
        prev = jnp.where(row == 0, prev_ref[slot:slot + 1, :width], pltpu.roll(x, 1, 0))
        prev_ref[slot:slot + 1, :width] = x[c - 1:c, :]
        return x + (prev - x) * mu_ref[...]

    r = shifted(r_ref, mu_r_ref, 0)
    k = shifted(k_ref, mu_k_ref, 1)
    v = shifted(v_ref, mu_v_ref, 2)
    gate = shifted(g_ref, mu_g_ref, 3)
    lora = shifted(lora_ref, mu_l_ref, 4)

    log_w = -_softplus(-(w0_ref[...] + _mm(jnp.tanh(lora), w_up_ref[...], na=n_lora, nb=n_lora))) - 0.5
    log_decay = -jnp.exp(log_w)
    iclr = _sigmoid(a0_ref[...] + _mm(lora, a_up_ref[...], na=n_lora, nb=n_lora))
    kk = k * kk_ref[...]
    kk = kk / jnp.maximum(jnp.sqrt(head_sums(kk * kk)), 1e-12)
    k = k * (1.0 + (iclr - 1.0) * ka_ref[...])

    r_i = lax.broadcasted_iota(jnp.int32, (c, c), 0)
    c_i = lax.broadcasted_iota(jnp.int32, (c, c), 1)
    lower = r_i >= c_i
    strict = r_i > c_i
    lower_2 = (lax.broadcasted_iota(jnp.int32, (c, 2 * c), 0)
               >= lax.broadcasted_iota(jnp.int32, (c, 2 * c), 1) % c)
    cum = _mm(lower.astype(BF16), log_decay, nb=n_exact)
    total = cum[c - 1:c, :]
    w_incl = jnp.exp(cum)
    w_excl = jnp.exp(cum - log_decay)
    w_inv = jnp.exp(-cum)
    w_rest = jnp.exp(total - cum)
    w_all = jnp.exp(total)

    kk_a = kk * iclr
    a_all = -kk * w_excl
    r_all = r * w_incl
    b_all = kk_a * w_inv
    k_all = k * w_inv
    bw_all = kk_a * w_rest
    kw_all = k * w_rest

    pairs = range(n_pairs)
    a_ts = [pair(a_all, p) for p in pairs]
    r_ts = [pair(r_all, p) for p in pairs]
    ms = []
    for p in pairs:
        lhs = jnp.concatenate([jnp.where(head0, a_ts[p], 0.0), jnp.where(head0, 0.0, a_ts[p]),
                               jnp.where(head0, r_ts[p], 0.0), jnp.where(head0, 0.0, r_ts[p])], axis=0)
        rhs = jnp.concatenate([pair(b_all, p), pair(k_all, p)], axis=0)
        ms.append(_mm(lhs, rhs, NT, na=n_scores, nb=n_scores))
    invs = _unit_lower_inverses([jnp.where(strict, ms[p][h * c:(h + 1) * c, :c], 0.0)
                                 for p in pairs for h in range(2)])
    states = [state_ref[p] for p in pairs]
    state_ts = [_bf16_terms(s, n_apply) for s in states]
    v_ts = [_bf16_terms(pair(v, p), n_apply) for p in pairs]
    x0_ts = [_bf16_terms(
        _mm(a_ts[p], state_ts[p], NT, na=n_apply) + jnp.where(
            head0,
            _mm(jnp.where(strict, ms[p][0:c, c:], 0.0), v_ts[p], na=n_apply),
            _mm(jnp.where(strict, ms[p][c:2 * c, c:], 0.0), v_ts[p], na=n_apply)), n_apply)
        for p in pairs]
    us = [jnp.where(head0, _mm(invs[2 * p], x0_ts[p], na=n_apply), _mm(invs[2 * p + 1], x0_ts[p], na=n_apply))
          for p in pairs]
    uv_ts = [_bf16_terms(jnp.concatenate([us[p], pair(v, p)], axis=0), n_apply) for p in pairs]
    ys = [_mm(r_ts[p], state_ts[p], NT, na=n_apply) + jnp.where(
        head0,
        _mm(jnp.where(lower_2, ms[p][2 * c:3 * c, :], 0.0), uv_ts[p], na=n_apply),
        _mm(jnp.where(lower_2, ms[p][3 * c:, :], 0.0), uv_ts[p], na=n_apply)) for p in pairs]
    for p in pairs:
        decayed = jnp.concatenate([pair(bw_all, p), pair(kw_all, p)], axis=0)
        state_ref[p] = states[p] * pair(w_all, p) + ones_h * _mm(uv_ts[p], decayed, TN, nb=n_apply)

    y = jnp.concatenate(ys, axis=1)
    mean = head_sums(y) * (1.0 / HEAD_DIM)
    yc = y - mean
    var = head_sums(yc * yc) * (1.0 / HEAD_DIM)
    yn = yc * lax.rsqrt(var + GN_EPS) * gng_ref[...] + gnb_ref[...]
    bonus = head_sums(r * k * rk_ref[...]) * v
    o_ref[0] = ((yn + bonus) * _silu(gate)).astype(o_ref.dtype)


def _rwkv(proj, mu_a, w_up, w0, a_up, a0, k_k, k_a, r_k, gn_g, gn_b, *, chunk=128, pairs_per_step=6):
    b, s, _ = proj.shape
    zeros = jnp.zeros((LORA_RANK, RWKV_WIDTH), F32)
    w_up_pad = jnp.concatenate([w_up, zeros], axis=0)
    a_up_pad = jnp.concatenate([zeros, a_up], axis=0)
    mu = mu_a.reshape(1, A_COLS)
    row = lambda x: x.reshape(1, RWKV_WIDTH)
    width = pairs_per_step * PAIR
    groups = N_PAIRS // pairs_per_step

    def act(section):
        return pl.BlockSpec((1, chunk, width), lambda bi, p, ci: (bi, ci, section * groups + p))

    def mu_spec(section):
        return pl.BlockSpec((1, width), lambda bi, p, ci: (0, section * groups + p))

    lora_spec = pl.BlockSpec((1, chunk, PAIR), lambda bi, p, ci: (bi, ci, LORA_BLK))
    mu_lora_spec = pl.BlockSpec((1, PAIR), lambda bi, p, ci: (0, LORA_BLK))
    up_spec = pl.BlockSpec((PAIR, width), lambda bi, p, ci: (0, p))
    vec = pl.BlockSpec((1, width), lambda bi, p, ci: (0, p))
    return pl.pallas_call(
        _rwkv_kernel,
        grid=(b, groups, s // chunk),
        in_specs=[act(0), act(1), act(2), act(3), lora_spec,
                  mu_spec(0), mu_spec(1), mu_spec(2), mu_spec(3), mu_lora_spec,
                  up_spec, up_spec] + [vec] * 7,
        out_specs=pl.BlockSpec((1, chunk, width), lambda bi, p, ci: (bi, ci, p)),
        out_shape=jax.ShapeDtypeStruct((b, s, RWKV_WIDTH), BF16),
        scratch_shapes=[pltpu.VMEM((pairs_per_step, PAIR, PAIR), F32), pltpu.VMEM((8, width), F32)],
        compiler_params=_params(("parallel", "parallel", "arbitrary")),
        name="rwkv7",
    )(proj, proj, proj, proj, proj, mu, mu, mu, mu, mu, w_up_pad, a_up_pad,
      row(w0), row(a0), row(k_k), row(k_a), row(r_k), row(gn_g), row(gn_b))


def kernel(x, norm_g, w_in, mu_a, w_up, w0, a_up, a0, k_k, k_a, r_k, gn_g, gn_b, pool_w, pool_scale,
           qn_g, kn_g, w_out):
    b, s, d = x.shape
    depth = w_in.shape[0]
    x2d = x.reshape(b * s, d)
    for l in range(depth):
        proj = _in_proj(x2d, norm_g[l].reshape(1, d), w_in[l].astype(BF16)).reshape(b, s, IN_COLS)
        ya = _rwkv(proj, mu_a[l], w_up[l], w0[l], a_up[l], a0[l], k_k[l], k_a[l], r_k[l], gn_g[l], gn_b[l])
        yb = _pool(proj, pool_w[l], pool_scale[l].reshape(1, POOL_WIDTH))
        yc = _stick_breaking(proj, qn_g[l], kn_g[l])
        w_o = w_out[l].astype(BF16)
        x2d = _out_proj(ya.reshape(b * s, RWKV_WIDTH), yb.reshape(b * s, POOL_WIDTH),
                        yc.reshape(b * s, SB_WIDTH),
                        w_o[:RWKV_WIDTH], w_o[RWKV_WIDTH:RWKV_WIDTH + POOL_WIDTH],
                        w_o[RWKV_WIDTH + POOL_WIDTH:], x2d)
    return x2d.reshape(b, s, d)
```

```python
import functools

import jax
import jax.numpy as jnp
from jax import lax
from jax.experimental import pallas as pl
from jax.experimental.pallas import tpu as pltpu

F32 = jnp.float32
BF16 = jnp.bfloat16

D_MODEL = 2048
HEAD_DIM = 64
PAIR = 2 * HEAD_DIM
RWKV_WIDTH = 768
POOL_WIDTH = 512
SB_WIDTH = 768
LORA_RANK = 64
POOL_WINDOWS = (2, 4, 8, 16)
POOL_GROUP_WIDTH = POOL_WIDTH // len(POOL_WINDOWS)
POOL_HALO = 16
RMS_EPS = 1e-6
GN_EPS = 64e-5
SB_ROW_GROUP = 128
SB_UNDERFLOW_LOG2 = 127.0
SB_MASKED = -1e30
SB_PREP_PAIRS = 3
SB_NORM_TERMS = 3
LOG2_E = 1.4426950408889634
A_COLS = 4 * RWKV_WIDTH + 2 * LORA_RANK
B_COLS = 2 * POOL_WIDTH
C_COLS = 4 * SB_WIDTH
IN_COLS = A_COLS + B_COLS + C_COLS
N_PAIRS = RWKV_WIDTH // PAIR

A_BLK = 0
LORA_BLK = 4 * RWKV_WIDTH // PAIR
B_BLK = A_COLS // PAIR
C_BLK = (A_COLS + B_COLS) // PAIR

VMEM_LIMIT = 56 * 1024 * 1024

NN = (((1,), (0,)), ((), ()))
NT = (((1,), (1,)), ((), ()))
TN = (((0,), (0,)), ((), ()))


def _dot32(a, b, dims=NN):
    return lax.dot_general(a, b, dims, precision=lax.Precision.HIGHEST, preferred_element_type=F32)


def _dot16(a, b, dims=NN):
    return lax.dot_general(a.astype(BF16), b.astype(BF16), dims, preferred_element_type=F32)


def _bf16_terms(x, n):
    terms = []
    for _ in range(n - 1):
        t = x.astype(BF16)
        terms.append(t)
        x = x - t.astype(F32)
    terms.append(x.astype(BF16))
    return terms


def _mm(a, b, dims=NN, na=1, nb=1):
    a_terms = a if isinstance(a, list) else _bf16_terms(a, na)
    b_terms = b if isinstance(b, list) else _bf16_terms(b, nb)
    keep = max(len(a_terms), len(b_terms))
    out = None
    for i, at in enumerate(a_terms):
        for j, bt in enumerate(b_terms):
            if i + j < keep:
                term = lax.dot_general(at, bt, dims, preferred_element_type=F32)
                out = term if out is None else out + term
    return out


def _sigmoid(x):
    return 1.0 / (1.0 + jnp.exp(-x))


def _silu(x):
    return x * _sigmoid(x)


def _softplus(x):
    return jnp.maximum(x, 0.0) + jnp.log(1.0 + jnp.exp(-jnp.abs(x)))


def _head_block_ones():
    r = lax.broadcasted_iota(jnp.int32, (PAIR, PAIR), 0) // HEAD_DIM
    c = lax.broadcasted_iota(jnp.int32, (PAIR, PAIR), 1) // HEAD_DIM
    return (r == c).astype(F32)


def _params(sem):
    return pltpu.CompilerParams(dimension_semantics=sem, vmem_limit_bytes=VMEM_LIMIT)


def _in_proj_kernel(x_ref, g_ref, w_ref, o_ref, h_ref, *, row_chunk):
    @pl.when(pl.program_id(1) == 0)
    def _norm():
        def body(c, carry):
            rows = pl.ds(pl.multiple_of(c * row_chunk, row_chunk), row_chunk)
            x = x_ref[rows, :]
            ms = jnp.mean(x * x, axis=-1, keepdims=True)
            h_ref[rows, :] = (x * lax.rsqrt(ms + RMS_EPS) * g_ref[...]).astype(BF16)
            return carry
        lax.fori_loop(0, x_ref.shape[0] // row_chunk, body, 0)

    o_ref[...] = jnp.dot(h_ref[...], w_ref[...], preferred_element_type=F32)


def _in_proj(x2d, g, w_bf16, *, tm=512, tn=2432):
    t, d = x2d.shape
    n = w_bf16.shape[1]
    return pl.pallas_call(
        functools.partial(_in_proj_kernel, row_chunk=128),
        grid=(t // tm, n // tn),
        in_specs=[
            pl.BlockSpec((tm, d), lambda i, j: (i, 0)),
            pl.BlockSpec((1, d), lambda i, j: (0, 0)),
            pl.BlockSpec((d, tn), lambda i, j: (0, j)),
        ],
        out_specs=pl.BlockSpec((tm, tn), lambda i, j: (i, j)),
        out_shape=jax.ShapeDtypeStruct((t, n), F32),
        scratch_shapes=[pltpu.VMEM((tm, d), BF16)],
        compiler_params=_params(("parallel", "arbitrary")),
        name="in_proj",
    )(x2d, g, w_bf16)


def _out_proj_kernel(ya_ref, yb_ref, yc_ref, wa_ref, wb_ref, wc_ref, x_ref, o_ref):
    acc = jnp.dot(ya_ref[...], wa_ref[...], preferred_element_type=F32)
    acc += jnp.dot(yb_ref[...], wb_ref[...], preferred_element_type=F32)
    acc += jnp.dot(yc_ref[...], wc_ref[...], preferred_element_type=F32)
    o_ref[...] = x_ref[...] + acc


def _out_proj(ya, yb, yc, wa, wb, wc, x2d, *, tm=512):
    t, d = x2d.shape
    row = lambda i: (i, 0)
    whole = lambda i: (0, 0)
    return pl.pallas_call(
        _out_proj_kernel,
        grid=(t // tm,),
        in_specs=[
            pl.BlockSpec((tm, RWKV_WIDTH), row),
            pl.BlockSpec((tm, POOL_WIDTH), row),
            pl.BlockSpec((tm, SB_WIDTH), row),
            pl.BlockSpec((RWKV_WIDTH, d), whole),
            pl.BlockSpec((POOL_WIDTH, d), whole),
            pl.BlockSpec((SB_WIDTH, d), whole),
            pl.BlockSpec((tm, d), row),
        ],
        out_specs=pl.BlockSpec((tm, d), row),
        out_shape=jax.ShapeDtypeStruct((t, d), F32),
        compiler_params=_params(("parallel",)),
        name="out_proj",
    )(ya, yb, yc, wa, wb, wc, x2d)


def _pool_kernel(*refs, ts):
    n_g = len(POOL_WINDOWS)
    cur_refs = refs[0:n_g]
    halo_refs = refs[n_g:2 * n_g]
    gate_refs = refs[2 * n_g:3 * n_g]
    w_ref, scale_ref, o_ref, ext_ref = refs[3 * n_g:]
    i = pl.program_id(1)
    pos = (i * ts + 1 + lax.broadcasted_iota(jnp.int32, (ts, 1), 0)).astype(F32)
    for gi, win in enumerate(POOL_WINDOWS):
        cur = cur_refs[gi][0]
        ext_ref[0:POOL_HALO, :] = jnp.where(i == 0, 0.0, halo_refs[gi][0])
        ext_ref[POOL_HALO:, :] = cur
        acc = cur
        for lag in range(1, win):
            acc = acc + ext_ref[POOL_HALO - lag:POOL_HALO - lag + ts, :]
        d = acc / jnp.minimum(pos, float(win)) - cur
        y = _dot16(d, w_ref[gi])
        cols = slice(gi * POOL_GROUP_WIDTH, (gi + 1) * POOL_GROUP_WIDTH)
        y = y * scale_ref[:, cols] * _silu(gate_refs[gi][0])
        o_ref[0, :, cols] = y.astype(o_ref.dtype)


def _pool(proj, pool_w, pool_scale, *, ts=512):
    b, s, _ = proj.shape
    n_g = len(POOL_WINDOWS)
    halo_blocks = ts // POOL_HALO
    cur_specs = [pl.BlockSpec((1, ts, POOL_GROUP_WIDTH), lambda bi, i, g=g: (bi, i, B_BLK + g))
                 for g in range(n_g)]
    halo_specs = [pl.BlockSpec((1, POOL_HALO, POOL_GROUP_WIDTH),
                               lambda bi, i, g=g: (bi, jnp.maximum(i * halo_blocks - 1, 0), B_BLK + g))
                  for g in range(n_g)]
    gate_specs = [pl.BlockSpec((1, ts, POOL_GROUP_WIDTH), lambda bi, i, g=g: (bi, i, B_BLK + n_g + g))
                  for g in range(n_g)]
    return pl.pallas_call(
        functools.partial(_pool_kernel, ts=ts),
        grid=(b, s // ts),
        in_specs=cur_specs + halo_specs + gate_specs + [
            pl.BlockSpec((n_g, POOL_GROUP_WIDTH, POOL_GROUP_WIDTH), lambda bi, i: (0, 0, 0)),
            pl.BlockSpec((1, POOL_WIDTH), lambda bi, i: (0, 0)),
        ],
        out_specs=pl.BlockSpec((1, ts, POOL_WIDTH), lambda bi, i: (bi, i, 0)),
        out_shape=jax.ShapeDtypeStruct((b, s, POOL_WIDTH), BF16),
        scratch_shapes=[pltpu.VMEM((ts + POOL_HALO, POOL_GROUP_WIDTH), F32)],
        compiler_params=_params(("parallel", "arbitrary")),
        name="pool",
    )(*([proj] * (3 * n_g)), pool_w, pool_scale)


def _sb_prep_kernel(q_ref, k_ref, v_ref, qg_ref, kg_ref, qn_ref, kn_ref, vb_ref):
    ones_h = _head_block_ones().astype(BF16)

    def normed(x, gain):
        ms = _mm(x * x, ones_h, na=SB_NORM_TERMS) * (1.0 / HEAD_DIM)
        return x * lax.rsqrt(ms + RMS_EPS) * gain

    for p in range(q_ref.shape[2] // PAIR):
        lanes = slice(p * PAIR, (p + 1) * PAIR)
        qn_ref[0, :, lanes] = (normed(q_ref[0, :, lanes], qg_ref[...]) * (HEAD_DIM ** -0.5 * LOG2_E)).astype(BF16)
        kn_ref[0, :, lanes] = normed(k_ref[0, :, lanes], kg_ref[...]).astype(BF16)
    vb_ref[0] = v_ref[0].astype(BF16)


def _sb_prep(proj, qn_g, kn_g, *, ts=512):
    b, s, _ = proj.shape
    width = SB_PREP_PAIRS * PAIR
    groups = SB_WIDTH // width
    first = C_BLK * PAIR // width
    qg = jnp.tile(qn_g.reshape(1, HEAD_DIM), (1, 2))
    kg = jnp.tile(kn_g.reshape(1, HEAD_DIM), (1, 2))
    vec = pl.BlockSpec((1, PAIR), lambda bi, i, j: (0, 0))
    out = pl.BlockSpec((1, ts, width), lambda bi, i, j: (bi, i, j))
    shape = jax.ShapeDtypeStruct((b, s, SB_WIDTH), BF16)
    return pl.pallas_call(
        _sb_prep_kernel,
        grid=(b, s // ts, groups),
        in_specs=[pl.BlockSpec((1, ts, width), lambda bi, i, j, sec=sec: (bi, i, first + sec * groups + j))
                  for sec in range(3)] + [vec, vec],
        out_specs=[out, out, out],
        out_shape=[shape, shape, shape],
        compiler_params=_params(("parallel", "parallel", "parallel")),
        name="sb_prep",
    )(proj, proj, proj, qg, kg)


def _sb_kernel(*refs, blk, n_pairs):
    qn_ref, kn_ref, vb_ref = refs[0:3]
    gate_refs = refs[3:3 + n_pairs]
    suffix_ref, future_ref, o_ref = refs[3 + n_pairs:]
    i = pl.program_id(2)
    head0 = lax.broadcasted_iota(jnp.int32, (1, PAIR), 1) < HEAD_DIM
    pairs = range(n_pairs)

    def lanes(p):
        return slice(p * PAIR, (p + 1) * PAIR)

    zero = jnp.zeros((), BF16)
    qqs = [jnp.concatenate([jnp.where(head0, qn_ref[0, :, lanes(p)], zero),
                            jnp.where(head0, zero, qn_ref[0, :, lanes(p)])], axis=0) for p in pairs]

    n_groups = 2 * blk // SB_ROW_GROUP
    groups = range(n_groups)

    def group(x, g):
        return x[g * SB_ROW_GROUP:(g + 1) * SB_ROW_GROUP]

    def walk(blocks, carries, accs):
        waves = [(b, p) for b in range(len(blocks)) for p in pairs]
        key_rows = [pl.ds(pl.multiple_of(j * blk, blk), blk) for j, _, _ in blocks]
        zs = {(b, p): [lax.dot_general(group(qqs[p], g), kn_ref[0, key_rows[b], lanes(p)], NT,
                                       preferred_element_type=F32) for g in groups] for b, p in waves}
        carries, accs = list(carries), list(accs)
        sums = {}

        def suffix_sums(b, p, anchor):
            if blocks[b][1]:
                zs[b, p] = [z + future_ref[g * SB_ROW_GROUP:(g + 1) * SB_ROW_GROUP, :]
                            for g, z in zip(groups, zs[b, p])]
            one = 1.0 if anchor is None else 1.0 + 0.0 * anchor
            drops = [jnp.maximum(z, 0.0) + jnp.log2(one + jnp.exp2(-jnp.abs(z))) for z in zs[b, p]]
            his = [x.astype(BF16) for x in drops]
            los = [(x - hi.astype(F32)).astype(BF16) for x, hi in zip(drops, his)]
            sums[b, p] = [jnp.dot(jnp.concatenate([hi, lo], axis=1), suffix_ref[...],
                                  preferred_element_type=F32) for hi, lo in zip(his, los)]

        def weigh(b, p):
            csums = [s + group(carries[p], g) for g, s in zip(groups, sums[b, p])]
            ws = [jnp.exp2(z - c) for z, c in zip(zs[b, p], csums)]
            if blocks[b][2] is not None:
                ws = [jnp.where(blocks[b][2], w, 0.0) for w in ws]
            vb = vb_ref[0, key_rows[b], lanes(p)]
            pvs = [jnp.dot(w.astype(BF16), vb, preferred_element_type=F32) for w in ws]
            pv0 = jnp.concatenate(pvs[:n_groups // 2], axis=0)
            pv1 = jnp.concatenate(pvs[n_groups // 2:], axis=0)
            accs[p] = accs[p] + jnp.where(head0, pv0, pv1)
            carries[p] = jnp.concatenate([c[:, 0:1] for c in csums], axis=0)

        suffix_sums(*waves[0], None)
        anchor = None
        for nxt, cur in zip(waves[1:], waves[:-1]):
            suffix_sums(*nxt, anchor)
            weigh(*cur)
            anchor = carries[cur[1]][0:1, 0:1]
        weigh(*waves[-1])
        return tuple(carries), tuple(accs)

    def live(carries):
        lowest = carries[0]
        for c in carries[1:]:
            lowest = jnp.minimum(lowest, c)
        return jnp.min(lowest) < SB_UNDERFLOW_LOG2

    carries, accs = walk([(i, True, None), (jnp.maximum(i - 1, 0), False, i > 0)],
                         tuple(jnp.zeros((2 * blk, 1), F32) for _ in pairs),
                         tuple(jnp.zeros((blk, PAIR), F32) for _ in pairs))

    def more(state):
        return jnp.logical_and(state[0] < i, state[1])

    def body(state):
        n, _, carries, accs = state
        carries, accs = walk([(i - 1 - n, False, None)], carries, accs)
        return n + 1, live(carries), carries, accs

    _, _, carries, accs = lax.while_loop(more, body, (jnp.int32(1), live(carries), carries, accs))
    for p in pairs:
        o_ref[0, :, lanes(p)] = (accs[p] * _silu(gate_refs[p][0])).astype(o_ref.dtype)


def _stick_breaking(proj, qn_g, kn_g, *, blk=256, pairs_per_step=3):
    b, s, _ = proj.shape
    qn, kn, vb = _sb_prep(proj, qn_g, kn_g)
    r_i = lax.broadcasted_iota(jnp.int32, (2 * blk, blk), 0) % blk
    c_i = lax.broadcasted_iota(jnp.int32, (2 * blk, blk), 1)
    suffix = (r_i >= c_i).astype(BF16)
    future = jnp.where(c_i < r_i, 0.0, SB_MASKED).astype(F32)
    width = pairs_per_step * PAIR
    gate_blk = C_BLK + 3 * N_PAIRS
    const = pl.BlockSpec((2 * blk, blk), lambda bi, p, i: (0, 0))
    keys = pl.BlockSpec((1, s, width), lambda bi, p, i: (bi, 0, p))
    gates = [pl.BlockSpec((1, blk, PAIR), lambda bi, p, i, k=k: (bi, i, gate_blk + p * pairs_per_step + k))
             for k in range(pairs_per_step)]
    return pl.pallas_call(
        functools.partial(_sb_kernel, blk=blk, n_pairs=pairs_per_step),
        grid=(b, N_PAIRS // pairs_per_step, s // blk),
        in_specs=[pl.BlockSpec((1, blk, width), lambda bi, p, i: (bi, i, p)), keys, keys] + gates + [const, const],
        out_specs=pl.BlockSpec((1, blk, width), lambda bi, p, i: (bi, i, p)),
        out_shape=jax.ShapeDtypeStruct((b, s, SB_WIDTH), BF16),
        compiler_params=_params(("parallel", "parallel", "arbitrary")),
        name="stick_breaking",
    )(qn, kn, vb, *([proj] * pairs_per_step), suffix, future)


INV_BLOCK = 16

RWKV_TERMS = {
    "lora": 2,
    "exact": 3,
    "scores": 1,
    "inverse": 1,
    "apply": 1,
}


def _unit_lower_inverses(lows):
    c = lows[0].shape[0]
    r_i = lax.broadcasted_iota(jnp.int32, (c, c), 0)
    c_i = lax.broadcasted_iota(jnp.int32, (c, c), 1)
    eye = (r_i == c_i).astype(F32)
    same_block = (r_i // INV_BLOCK) == (c_i // INV_BLOCK)
    diags = [jnp.where(same_block, low, 0.0) for low in lows]
    offs = [jnp.where(same_block, 0.0, low) for low in lows]

    n_inv = RWKV_TERMS["inverse"]

    def neumann(ns, order):
        invs = [eye + n for n in ns]
        powers = [_mm(t, t) for t in (_bf16_terms(n, n_inv) for n in ns)]
        span = 2
        while 2 * span < order:
            stacked = [_mm(jnp.concatenate([inv, pw], axis=0), pw, na=n_inv, nb=n_inv)
                       for inv, pw in zip(invs, powers)]
            invs = [inv + st[:c] for inv, st in zip(invs, stacked)]
            powers = [st[c:] for st in stacked]
            span *= 2
        return [inv + _mm(inv, pw, na=n_inv, nb=n_inv) for inv, pw in zip(invs, powers)]

    inv_diags = [_bf16_terms(inv, n_inv) for inv in neumann(diags, INV_BLOCK)]
    inv_offs = neumann([_mm(inv, off, nb=n_inv) for inv, off in zip(inv_diags, offs)], c // INV_BLOCK)
    return [_mm(inv_off, inv_diag, na=n_inv) for inv_off, inv_diag in zip(inv_offs, inv_diags)]


def _rwkv_kernel(r_ref, k_ref, v_ref, g_ref, lora_ref,
                 mu_r_ref, mu_k_ref, mu_v_ref, mu_g_ref, mu_l_ref,
                 w_up_ref, a_up_ref, w0_ref, a0_ref, kk_ref, ka_ref, rk_ref, gng_ref, gnb_ref,
                 o_ref, state_ref, prev_ref):
    ci = pl.program_id(2)
    c = r_ref.shape[1]
    n_pairs = r_ref.shape[2] // PAIR
    n_lora, n_exact, n_scores, n_apply = (RWKV_TERMS[s] for s in ("lora", "exact", "scores", "apply"))

    @pl.when(ci == 0)
    def _reset():
        state_ref[...] = jnp.zeros_like(state_ref)
        prev_ref[...] = jnp.zeros_like(prev_ref)

    row = lax.broadcasted_iota(jnp.int32, (c, 1), 0)
    head0 = lax.broadcasted_iota(jnp.int32, (1, PAIR), 1) < HEAD_DIM
    ones_h = _head_block_ones()
    ones_h16 = ones_h.astype(BF16)

    def pair(x, p):
        return x[:, p * PAIR:(p + 1) * PAIR]

    def head_sums(x):
        r2 = lax.broadcasted_iota(jnp.int32, (2 * PAIR, 2 * PAIR), 0) // HEAD_DIM
        c2 = lax.broadcasted_iota(jnp.int32, (2 * PAIR, 2 * PAIR), 1) // HEAD_DIM
        ones_2 = (r2 == c2).astype(BF16)
        parts = [_mm(x[:, p * PAIR:(p + 2) * PAIR], ones_2, na=n_exact) for p in range(0, n_pairs - 1, 2)]
        if n_pairs % 2:
            parts.append(_mm(pair(x, n_pairs - 1), ones_h16, na=n_exact))
        return jnp.concatenate(parts, axis=1)

    def shifted(ref, mu_ref, slot):
        x = ref[0]
        width = x.shape[1]
        prev = jnp.where(row == 0, prev_ref[slot:slot + 1, :width], pltpu.roll(x, 1, 0))
        prev_ref[slot:slot + 1, :width] = x[c - 1:c, :]
        return x + (prev - x) * mu_ref[...]

    r = shifted(r_ref, mu_r_ref, 0)
    k = shifted(k_ref, mu_k_ref, 1)
    v = shifted(v_ref, mu_v_ref, 2)
    gate = shifted(g_ref, mu_g_ref, 3)
    lora = shifted(lora_ref, mu_l_ref, 4)

    log_w = -_softplus(-(w0_ref[...] + _mm(jnp.tanh(lora), w_up_ref[...], na=n_lora, nb=n_lora))) - 0.5
    log_decay = -jnp.exp(log_w)
    iclr = _sigmoid(a0_ref[...] + _mm(lora, a_up_ref[...], na=n_lora, nb=n_lora))
    kk = k * kk_ref[...]
    kk = kk / jnp.maximum(jnp.sqrt(head_sums(kk * kk)), 1e-12)
    k = k * (1.0 + (iclr - 1.0) * ka_ref[...])

    r_i = lax.broadcasted_iota(jnp.int32, (c, c), 0)
    c_i = lax.broadcasted_iota(jnp.int32, (c, c), 1)
    lower = r_i >= c_i
    strict = r_i > c_i
    lower_2 = (lax.broadcasted_iota(jnp.int32, (c, 2 * c), 0)
               >= lax.broadcasted_iota(jnp.int32, (c, 2 * c), 1) % c)
    cum = _mm(lower.astype(BF16), log_decay, nb=n_exact)
    total = cum[c - 1:c, :]
    w_incl = jnp.exp(cum)
    w_excl = jnp.exp(cum - log_decay)
    w_inv = jnp.exp(-cum)
    w_rest = jnp.exp(total - cum)
    w_all = jnp.exp(total)

    kk_a = kk * iclr
    a_all = -kk * w_excl
    r_all = r * w_incl
    b_all = kk_a * w_inv
    k_all = k * w_inv
    bw_all = kk_a * w_rest
    kw_all = k * w_rest

    pairs = range(n_pairs)
    a_ts = [pair(a_all, p) for p in pairs]
    r_ts = [pair(r_all, p) for p in pairs]
    ms = []
    for p in pairs:
        lhs = jnp.concatenate([jnp.where(head0, a_ts[p], 0.0), jnp.where(head0, 0.0, a_ts[p]),
                               jnp.where(head0, r_ts[p], 0.0), jnp.where(head0, 0.0, r_ts[p])], axis=0)
        rhs = jnp.concatenate([pair(b_all, p), pair(k_all, p)], axis=0)
        ms.append(_mm(lhs, rhs, NT, na=n_scores, nb=n_scores))
    invs = _unit_lower_inverses([jnp.where(strict, ms[p][h * c:(h + 1) * c, :c], 0.0)
                                 for p in pairs for h in range(2)])
    states = [state_ref[p] for p in pairs]
    state_ts = [_bf16_terms(s, n_apply) for s in states]
    v_ts = [_bf16_terms(pair(v, p), n_apply) for p in pairs]
    x0_ts = [_bf16_terms(
        _mm(a_ts[p], state_ts[p], NT, na=n_apply) + jnp.where(
            head0,
            _mm(jnp.where(strict, ms[p][0:c, c:], 0.0), v_ts[p], na=n_apply),
            _mm(jnp.where(strict, ms[p][c:2 * c, c:], 0.0), v_ts[p], na=n_apply)), n_apply)
        for p in pairs]
    us = [jnp.where(head0, _mm(invs[2 * p], x0_ts[p], na=n_apply), _mm(invs[2 * p + 1], x0_ts[p], na=n_apply))
          for p in pairs]
    uv_ts = [_bf16_terms(jnp.concatenate([us[p], pair(v, p)], axis=0), n_apply) for p in pairs]
    ys = [_mm(r_ts[p], state_ts[p], NT, na=n_apply) + jnp.where(
        head0,
        _mm(jnp.where(lower_2, ms[p][2 * c:3 * c, :], 0.0), uv_ts[p], na=n_apply),
        _mm(jnp.where(lower_2, ms[p][3 * c:, :], 0.0), uv_ts[p], na=n_apply)) for p in pairs]
    for p in pairs:
        decayed = jnp.concatenate([pair(bw_all, p), pair(kw_all, p)], axis=0)
        state_ref[p] = states[p] * pair(w_all, p) + ones_h * _mm(uv_ts[p], decayed, TN, nb=n_apply)

    y = jnp.concatenate(ys, axis=1)
    mean = head_sums(y) * (1.0 / HEAD_DIM)
    yc = y - mean
    var = head_sums(yc * yc) * (1.0 / HEAD_DIM)
    yn = yc * lax.rsqrt(var + GN_EPS) * gng_ref[...] + gnb_ref[...]
    bonus = head_sums(r * k * rk_ref[...]) * v
    o_ref[0] = ((yn + bonus) * _silu(gate)).astype(o_ref.dtype)


def _rwkv(proj, mu_a, w_up, w0, a_up, a0, k_k, k_a, r_k, gn_g, gn_b, *, chunk=128, pairs_per_step=6):
    b, s, _ = proj.shape
    zeros = jnp.zeros((LORA_RANK, RWKV_WIDTH), F32)
    w_up_pad = jnp.concatenate([w_up, zeros], axis=0)
    a_up_pad = jnp.concatenate([zeros, a_up], axis=0)
    mu = mu_a.reshape(1, A_COLS)
    row = lambda x: x.reshape(1, RWKV_WIDTH)
    width = pairs_per_step * PAIR
    groups = N_PAIRS // pairs_per_step

    def act(section):
        return pl.BlockSpec((1, chunk, width), lambda bi, p, ci: (bi, ci, section * groups + p))

    def mu_spec(section):
        return pl.BlockSpec((1, width), lambda bi, p, ci: (0, section * groups + p))

    lora_spec = pl.BlockSpec((1, chunk, PAIR), lambda bi, p, ci: (bi, ci, LORA_BLK))
    mu_lora_spec = pl.BlockSpec((1, PAIR), lambda bi, p, ci: (0, LORA_BLK))
    up_spec = pl.BlockSpec((PAIR, width), lambda bi, p, ci: (0, p))
    vec = pl.BlockSpec((1, width), lambda bi, p, ci: (0, p))
    return pl.pallas_call(
        _rwkv_kernel,
        grid=(b, groups, s // chunk),
        in_specs=[act(0), act(1), act(2), act(3), lora_spec,
                  mu_spec(0), mu_spec(1), mu_spec(2), mu_spec(3), mu_lora_spec,
                  up_spec, up_spec] + [vec] * 7,
        out_specs=pl.BlockSpec((1, chunk, width), lambda bi, p, ci: (bi, ci, p)),
        out_shape=jax.ShapeDtypeStruct((b, s, RWKV_WIDTH), BF16),
        scratch_shapes=[pltpu.VMEM((pairs_per_step, PAIR, PAIR), F32), pltpu.VMEM((8, width), F32)],
        compiler_params=_params(("parallel", "parallel", "arbitrary")),
        name="rwkv7",
    )(proj, proj, proj, proj, proj, mu, mu, mu, mu, mu, w_up_pad, a_up_pad,
      row(w0), row(a0), row(k_k), row(k_a), row(r_k), row(gn_g), row(gn_b))


def kernel(x, norm_g, w_in, mu_a, w_up, w0, a_up, a0, k_k, k_a, r_k, gn_g, gn_b, pool_w, pool_scale,
           qn_g, kn_g, w_out):
    b, s, d = x.shape
    depth = w_in.shape[0]
    x2d = x.reshape(b * s, d)
    for l in range(depth):
        proj = _in_proj(x2d, norm_g[l].reshape(1, d), w_in[l].astype(BF16)).reshape(b, s, IN_COLS)
        ya = _rwkv(proj, mu_a[l], w_up[l], w0[l], a_up[l], a0[l], k_k[l], k_a[l], r_k[l], gn_g[l], gn_b[l])
        yb = _pool(proj, pool_w[l], pool_scale[l].reshape(1, POOL_WIDTH))
        yc = _stick_breaking(proj, qn_g[l], kn_g[l])
        w_o = w_out[l].astype(BF16)
        x2d = _out_proj(ya.reshape(b * s, RWKV_WIDTH), yb.reshape(b * s, POOL_WIDTH),
                        yc.reshape(b * s, SB_WIDTH),
                        w_o[:RWKV_WIDTH], w_o[RWKV_WIDTH:RWKV_WIDTH + POOL_WIDTH],
                        w_o[RWKV_WIDTH + POOL_WIDTH:], x2d)
    return x2d.reshape(b, s, d)
```

```python
import functools

import jax
import jax.numpy as jnp
from jax import lax
from jax.experimental import pallas as pl
from jax.experimental.pallas import tpu as pltpu

F32 = jnp.float32
BF16 = jnp.bfloat16

D_MODEL = 2048
HEAD_DIM = 64
PAIR = 2 * HEAD_DIM
RWKV_WIDTH = 768
POOL_WIDTH = 512
SB_WIDTH = 768
LORA_RANK = 64
POOL_WINDOWS = (2, 4, 8, 16)
POOL_GROUP_WIDTH = POOL_WIDTH // len(POOL_WINDOWS)
POOL_HALO = 16
RMS_EPS = 1e-6
GN_EPS = 64e-5
LOG2_E = 1.4426950408889634
A_COLS = 4 * RWKV_WIDTH + 2 * LORA_RANK
B_COLS = 2 * POOL_WIDTH
C_COLS = 4 * SB_WIDTH
IN_COLS = A_COLS + B_COLS + C_COLS
N_PAIRS = RWKV_WIDTH // PAIR

LORA_BLK = 4 * RWKV_WIDTH // PAIR
B_BLK = A_COLS // PAIR
C_BLK = (A_COLS + B_COLS) // PAIR

RWKV_CHUNK = 128
INV_BLOCK = 16
RWKV_TERMS = {
    "lora": 2,
    "exact": 3,
    "scores": 1,
    "inverse": 1,
    "apply": 1,
}

SB_PAIRS = 3
SB_BLOCK = RWKV_CHUNK * (N_PAIRS // SB_PAIRS)
SB_ROW_GROUP = 128
SB_UNDERFLOW_LOG2 = 127.0
SB_MASKED = -1e30
SB_NORM_TERMS = 3
SB_SUFFIX_TERMS = 1
SB_TAIL = "tail"

VMEM_LIMIT = 56 * 1024 * 1024

NN = (((1,), (0,)), ((), ()))
NT = (((1,), (1,)), ((), ()))
TN = (((0,), (0,)), ((), ()))


def _dot16(a, b, dims=NN):
    return lax.dot_general(a.astype(BF16), b.astype(BF16), dims, preferred_element_type=F32)


def _bf16_terms(x, n):
    terms = []
    for _ in range(n - 1):
        t = x.astype(BF16)
        terms.append(t)
        x = x - t.astype(F32)
    terms.append(x.astype(BF16))
    return terms


def _mm(a, b, dims=NN, na=1, nb=1):
    a_terms = a if isinstance(a, list) else _bf16_terms(a, na)
    b_terms = b if isinstance(b, list) else _bf16_terms(b, nb)
    keep = max(len(a_terms), len(b_terms))
    out = None
    for i, at in enumerate(a_terms):
        for j, bt in enumerate(b_terms):
            if i + j < keep:
                term = lax.dot_general(at, bt, dims, preferred_element_type=F32)
                out = term if out is None else out + term
    return out


def _sigmoid(x):
    return 1.0 / (1.0 + jnp.exp(-x))


def _silu(x):
    return x * _sigmoid(x)


def _softplus(x):
    return jnp.maximum(x, 0.0) + jnp.log(1.0 + jnp.exp(-jnp.abs(x)))


def _head_block_ones(width=PAIR):
    r = lax.broadcasted_iota(jnp.int32, (width, width), 0) // HEAD_DIM
    c = lax.broadcasted_iota(jnp.int32, (width, width), 1) // HEAD_DIM
    return (r == c).astype(F32)


def _params(sem):
    return pltpu.CompilerParams(dimension_semantics=sem, vmem_limit_bytes=VMEM_LIMIT)


def _in_proj_kernel(x_ref, g_ref, w_ref, o_ref, h_ref, *, row_chunk):
    @pl.when(pl.program_id(1) == 0)
    def _norm():
        def body(c, carry):
            rows = pl.ds(pl.multiple_of(c * row_chunk, row_chunk), row_chunk)
            x = x_ref[rows, :]
            ms = jnp.mean(x * x, axis=-1, keepdims=True)
            h_ref[rows, :] = (x * lax.rsqrt(ms + RMS_EPS) * g_ref[...]).astype(BF16)
            return carry
        lax.fori_loop(0, x_ref.shape[0] // row_chunk, body, 0)

    o_ref[...] = jnp.dot(h_ref[...], w_ref[...], preferred_element_type=F32)


def _in_proj(x2d, g, w_bf16, *, tm=512, tn=2432):
    t, d = x2d.shape
    n = w_bf16.shape[1]
    return pl.pallas_call(
        functools.partial(_in_proj_kernel, row_chunk=128),
        grid=(t // tm, n // tn),
        in_specs=[
            pl.BlockSpec((tm, d), lambda i, j: (i, 0)),
            pl.BlockSpec((1, d), lambda i, j: (0, 0)),
            pl.BlockSpec((d, tn), lambda i, j: (0, j)),
        ],
        out_specs=pl.BlockSpec((tm, tn), lambda i, j: (i, j)),
        out_shape=jax.ShapeDtypeStruct((t, n), F32),
        scratch_shapes=[pltpu.VMEM((tm, d), BF16)],
        compiler_params=_params(("parallel", "arbitrary")),
        name="in_proj",
    )(x2d, g, w_bf16)


def _out_proj_kernel(ya_ref, yb_ref, yc_ref, wa_ref, wb_ref, wc_ref, x_ref, o_ref):
    acc = jnp.dot(ya_ref[...], wa_ref[...], preferred_element_type=F32)
    acc += jnp.dot(yb_ref[...], wb_ref[...], preferred_element_type=F32)
    acc += jnp.dot(yc_ref[...], wc_ref[...], preferred_element_type=F32)
    o_ref[...] = x_ref[...] + acc


def _out_proj(ya, yb, yc, wa, wb, wc, x2d, *, tm=512):
    t, d = x2d.shape
    row = lambda i: (i, 0)
    whole = lambda i: (0, 0)
    return pl.pallas_call(
        _out_proj_kernel,
        grid=(t // tm,),
        in_specs=[
            pl.BlockSpec((tm, RWKV_WIDTH), row),
            pl.BlockSpec((tm, POOL_WIDTH), row),
            pl.BlockSpec((tm, SB_WIDTH), row),
            pl.BlockSpec((RWKV_WIDTH, d), whole),
            pl.BlockSpec((POOL_WIDTH, d), whole),
            pl.BlockSpec((SB_WIDTH, d), whole),
            pl.BlockSpec((tm, d), row),
        ],
        out_specs=pl.BlockSpec((tm, d), row),
        out_shape=jax.ShapeDtypeStruct((t, d), F32),
        compiler_params=_params(("parallel",)),
        name="out_proj",
    )(ya, yb, yc, wa, wb, wc, x2d)


def _pool_kernel(*refs, ts):
    n_g = len(POOL_WINDOWS)
    cur_refs = refs[0:n_g]
    halo_refs = refs[n_g:2 * n_g]
    gate_refs = refs[2 * n_g:3 * n_g]
    w_ref, scale_ref, o_ref, ext_ref = refs[3 * n_g:]
    i = pl.program_id(1)
    pos = (i * ts + 1 + lax.broadcasted_iota(jnp.int32, (ts, 1), 0)).astype(F32)
    for gi, win in enumerate(POOL_WINDOWS):
        cur = cur_refs[gi][0]
        ext_ref[0:POOL_HALO, :] = jnp.where(i == 0, 0.0, halo_refs[gi][0])
        ext_ref[POOL_HALO:, :] = cur
        acc = cur
        for lag in range(1, win):
            acc = acc + ext_ref[POOL_HALO - lag:POOL_HALO - lag + ts, :]
        d = acc / jnp.minimum(pos, float(win)) - cur
        y = _dot16(d, w_ref[gi])
        cols = slice(gi * POOL_GROUP_WIDTH, (gi + 1) * POOL_GROUP_WIDTH)
        y = y * scale_ref[:, cols] * _silu(gate_refs[gi][0])
        o_ref[0, :, cols] = y.astype(o_ref.dtype)


def _pool(proj, pool_w, pool_scale, *, ts=512):
    b, s, _ = proj.shape
    n_g = len(POOL_WINDOWS)
    halo_blocks = ts // POOL_HALO
    cur_specs = [pl.BlockSpec((1, ts, POOL_GROUP_WIDTH), lambda bi, i, g=g: (bi, i, B_BLK + g))
                 for g in range(n_g)]
    halo_specs = [pl.BlockSpec((1, POOL_HALO, POOL_GROUP_WIDTH),
                               lambda bi, i, g=g: (bi, jnp.maximum(i * halo_blocks - 1, 0), B_BLK + g))
                  for g in range(n_g)]
    gate_specs = [pl.BlockSpec((1, ts, POOL_GROUP_WIDTH), lambda bi, i, g=g: (bi, i, B_BLK + n_g + g))
                  for g in range(n_g)]
    return pl.pallas_call(
        functools.partial(_pool_kernel, ts=ts),
        grid=(b, s // ts),
        in_specs=cur_specs + halo_specs + gate_specs + [
            pl.BlockSpec((n_g, POOL_GROUP_WIDTH, POOL_GROUP_WIDTH), lambda bi, i: (0, 0, 0)),
            pl.BlockSpec((1, POOL_WIDTH), lambda bi, i: (0, 0)),
        ],
        out_specs=pl.BlockSpec((1, ts, POOL_WIDTH), lambda bi, i: (bi, i, 0)),
        out_shape=jax.ShapeDtypeStruct((b, s, POOL_WIDTH), BF16),
        scratch_shapes=[pltpu.VMEM((ts + POOL_HALO, POOL_GROUP_WIDTH), F32)],
        compiler_params=_params(("parallel", "arbitrary")),
        name="pool",
    )(*([proj] * (3 * n_g)), pool_w, pool_scale)


def _sb_prep_kernel(q_ref, k_ref, v_ref, qg_ref, kg_ref, qn_ref, kn_ref, vb_ref):
    ones_h = _head_block_ones().astype(BF16)

    def normed(x, gain):
        ms = _mm(x * x, ones_h, na=SB_NORM_TERMS) * (1.0 / HEAD_DIM)
        return x * lax.rsqrt(ms + RMS_EPS) * gain

    for p in range(q_ref.shape[2] // PAIR):
        lanes = slice(p * PAIR, (p + 1) * PAIR)
        qn_ref[0, :, lanes] = (normed(q_ref[0, :, lanes], qg_ref[...]) * (HEAD_DIM ** -0.5 * LOG2_E)).astype(BF16)
        kn_ref[0, :, lanes] = normed(k_ref[0, :, lanes], kg_ref[...]).astype(BF16)
    vb_ref[0] = v_ref[0].astype(BF16)


def _sb_prep(proj, qn_g, kn_g, *, ts=512):
    b, s, _ = proj.shape
    width = SB_PAIRS * PAIR
    groups = SB_WIDTH // width
    first = C_BLK * PAIR // width
    qg = jnp.tile(qn_g.reshape(1, HEAD_DIM), (1, 2))
    kg = jnp.tile(kn_g.reshape(1, HEAD_DIM), (1, 2))
    vec = pl.BlockSpec((1, PAIR), lambda bi, i, j: (0, 0))
    out = pl.BlockSpec((1, ts, width), lambda bi, i, j: (bi, i, j))
    shape = jax.ShapeDtypeStruct((b, s, SB_WIDTH), BF16)
    return pl.pallas_call(
        _sb_prep_kernel,
        grid=(b, s // ts, groups),
        in_specs=[pl.BlockSpec((1, ts, width), lambda bi, i, j, sec=sec: (bi, i, first + sec * groups + j))
                  for sec in range(3)] + [vec, vec],
        out_specs=[out, out, out],
        out_shape=[shape, shape, shape],
        compiler_params=_params(("parallel", "parallel", "parallel")),
        name="sb_prep",
    )(proj, proj, proj, qg, kg)


def _sb_phases(qn_ref, kn_ref, vb_ref, gate_refs, suffix_ref, future_ref, o_ref, *, i, blk):
    pairs = range(len(gate_refs))
    head0 = lax.broadcasted_iota(jnp.int32, (1, PAIR), 1) < HEAD_DIM

    def lanes(p):
        return slice(p * PAIR, (p + 1) * PAIR)

    zero = jnp.zeros((), BF16)
    qqs = [jnp.concatenate([jnp.where(head0, qn_ref[0, :, lanes(p)], zero),
                            jnp.where(head0, zero, qn_ref[0, :, lanes(p)])], axis=0) for p in pairs]

    n_groups = 2 * blk // SB_ROW_GROUP
    groups = range(n_groups)

    def group(x, g):
        return x[g * SB_ROW_GROUP:(g + 1) * SB_ROW_GROUP]

    def walk(blocks, state):
        waves = [(b, p) for b in range(len(blocks)) for p in pairs]
        key_rows = [pl.ds(pl.multiple_of(j * blk, blk), blk) for j, _, _ in blocks]
        zs = {(b, p): [lax.dot_general(group(qqs[p], g), kn_ref[0, key_rows[b], lanes(p)], NT,
                                       preferred_element_type=F32) for g in groups] for b, p in waves}
        yield
        sums = {}
        for b, p in waves:
            if blocks[b][1]:
                zs[b, p] = [z + future_ref[g * SB_ROW_GROUP:(g + 1) * SB_ROW_GROUP, :]
                            for g, z in zip(groups, zs[b, p])]
            drops = [jnp.maximum(z, 0.0) + jnp.log2(1.0 + jnp.exp2(-jnp.abs(z))) for z in zs[b, p]]
            sums[b, p] = [jnp.dot(jnp.concatenate(_bf16_terms(x, SB_SUFFIX_TERMS), axis=1), suffix_ref[...],
                                  preferred_element_type=F32) for x in drops]
            yield
        carries, accs = state["carries"], state["accs"]
        for b, p in waves:
            csums = [s + group(carries[p], g) for g, s in zip(groups, sums[b, p])]
            ws = [jnp.exp2(z - c) for z, c in zip(zs[b, p], csums)]
            if blocks[b][2] is not None:
                ws = [jnp.where(blocks[b][2], w, 0.0) for w in ws]
            vb = vb_ref[0, key_rows[b], lanes(p)]
            pvs = [jnp.dot(w.astype(BF16), vb, preferred_element_type=F32) for w in ws]
            pv0 = jnp.concatenate(pvs[:n_groups // 2], axis=0)
            pv1 = jnp.concatenate(pvs[n_groups // 2:], axis=0)
            accs[p] = accs[p] + jnp.where(head0, pv0, pv1)
            carries[p] = jnp.concatenate([c[:, 0:1] for c in csums], axis=0)
            yield

    def live(carries):
        lowest = carries[0]
        for c in carries[1:]:
            lowest = jnp.minimum(lowest, c)
        return jnp.min(lowest) < SB_UNDERFLOW_LOG2

    state = {"carries": [jnp.zeros((2 * blk, 1), F32) for _ in pairs],
             "accs": [jnp.zeros((blk, PAIR), F32) for _ in pairs]}
    yield from walk([(i, True, None), (jnp.maximum(i - 1, 0), False, i > 0)], state)
    yield SB_TAIL

    def more(loop):
        return jnp.logical_and(loop[0] < i, loop[1])

    def body(loop):
        n, _, carries, accs = loop
        state = {"carries": list(carries), "accs": list(accs)}
        for _ in walk([(i - 1 - n, False, None)], state):
            pass
        return n + 1, live(state["carries"]), tuple(state["carries"]), tuple(state["accs"])

    _, _, _, accs = lax.while_loop(
        more, body, (jnp.int32(1), live(state["carries"]), tuple(state["carries"]), tuple(state["accs"])))
    for p in pairs:
        o_ref[0, :, lanes(p)] = (accs[p] * _silu(gate_refs[p][0])).astype(o_ref.dtype)


def _unit_lower_inverses(lows):
    c = lows[0].shape[0]
    r_i = lax.broadcasted_iota(jnp.int32, (c, c), 0)
    c_i = lax.broadcasted_iota(jnp.int32, (c, c), 1)
    eye = (r_i == c_i).astype(F32)
    same_block = (r_i // INV_BLOCK) == (c_i // INV_BLOCK)
    diags = [jnp.where(same_block, low, 0.0) for low in lows]
    offs = [jnp.where(same_block, 0.0, low) for low in lows]

    n_inv = RWKV_TERMS["inverse"]

    def neumann(ns, order):
        invs = [eye + n for n in ns]
        powers = [_mm(t, t) for t in (_bf16_terms(n, n_inv) for n in ns)]
        yield
        span = 2
        while 2 * span < order:
            stacked = [_mm(jnp.concatenate([inv, pw], axis=0), pw, na=n_inv, nb=n_inv)
                       for inv, pw in zip(invs, powers)]
            invs = [inv + st[:c] for inv, st in zip(invs, stacked)]
            powers = [st[c:] for st in stacked]
            span *= 2
            yield
        return [inv + _mm(inv, pw, na=n_inv, nb=n_inv) for inv, pw in zip(invs, powers)]

    inv_diags = yield from neumann(diags, INV_BLOCK)
    inv_diags = [_bf16_terms(inv, n_inv) for inv in inv_diags]
    yield
    remainders = [_mm(inv, off, nb=n_inv) for inv, off in zip(inv_diags, offs)]
    yield
    inv_offs = yield from neumann(remainders, c // INV_BLOCK)
    yield
    return [_mm(inv_off, inv_diag, na=n_inv) for inv_off, inv_diag in zip(inv_offs, inv_diags)]


def _rwkv_phases(r_ref, k_ref, v_ref, g_ref, lora_ref,
                 mu_r_ref, mu_k_ref, mu_v_ref, mu_g_ref, mu_l_ref,
                 w_up_ref, a_up_ref, w0_ref, a0_ref, kk_ref, ka_ref, rk_ref, gng_ref, gnb_ref,
                 o_ref, state_ref, prev_ref):
    c = r_ref.shape[1]
    n_pairs = r_ref.shape[2] // PAIR
    n_lora, n_exact, n_scores, n_apply = (RWKV_TERMS[s] for s in ("lora", "exact", "scores", "apply"))

    row = lax.broadcasted_iota(jnp.int32, (c, 1), 0)
    head0 = lax.broadcasted_iota(jnp.int32, (1, PAIR), 1) < HEAD_DIM
    ones_h = _head_block_ones()
    ones_h16 = ones_h.astype(BF16)
    ones_2 = _head_block_ones(2 * PAIR).astype(BF16)

    def pair(x, p):
        return x[:, p * PAIR:(p + 1) * PAIR]

    def head_sums(x):
        parts = [_mm(x[:, p * PAIR:(p + 2) * PAIR], ones_2, na=n_exact) for p in range(0, n_pairs - 1, 2)]
        if n_pairs % 2:
            parts.append(_mm(pair(x, n_pairs - 1), ones_h16, na=n_exact))
        return jnp.concatenate(parts, axis=1)

    def shifted(ref, mu_ref, slot):
        x = ref[0]
        width = x.shape[1]
        prev = jnp.where(row == 0, prev_ref[slot:slot + 1, :width], pltpu.roll(x, 1, 0))
        prev_ref[slot:slot + 1, :width] = x[c - 1:c, :]
        return x + (prev - x) * mu_ref[...]

    r = shifted(r_ref, mu_r_ref, 0)
    k = shifted(k_ref, mu_k_ref, 1)
    v = shifted(v_ref, mu_v_ref, 2)
    gate = shifted(g_ref, mu_g_ref, 3)
    lora = shifted(lora_ref, mu_l_ref, 4)

    log_w = -_softplus(-(w0_ref[...] + _mm(jnp.tanh(lora), w_up_ref[...], na=n_lora, nb=n_lora))) - 0.5
    log_decay = -jnp.exp(log_w)
    iclr = _sigmoid(a0_ref[...] + _mm(lora, a_up_ref[...], na=n_lora, nb=n_lora))
    kk = k * kk_ref[...]
    kk = kk / jnp.maximum(jnp.sqrt(head_sums(kk * kk)), 1e-12)
    k = k * (1.0 + (iclr - 1.0) * ka_ref[...])

    r_i = lax.broadcasted_iota(jnp.int32, (c, c), 0)
    c_i = lax.broadcasted_iota(jnp.int32, (c, c), 1)
    lower = r_i >= c_i
    strict = r_i > c_i
    lower_2 = (lax.broadcasted_iota(jnp.int32, (c, 2 * c), 0)
               >= lax.broadcasted_iota(jnp.int32, (c, 2 * c), 1) % c)
    cum = _mm(lower.astype(BF16), log_decay, nb=n_exact)
    total = cum[c - 1:c, :]
    w_incl = jnp.exp(cum)
    w_excl = jnp.exp(cum - log_decay)
    w_inv = jnp.exp(-cum)
    w_rest = jnp.exp(total - cum)
    w_all = jnp.exp(total)

    kk_a = kk * iclr
    a_all = -kk * w_excl
    r_all = r * w_incl
    b_all = kk_a * w_inv
    k_all = k * w_inv
    bw_all = kk_a * w_rest
    kw_all = k * w_rest

    pairs = range(n_pairs)
    a_ts = [pair(a_all, p) for p in pairs]
    r_ts = [pair(r_all, p) for p in pairs]
    ms = []
    for p in pairs:
        lhs = jnp.concatenate([jnp.where(head0, a_ts[p], 0.0), jnp.where(head0, 0.0, a_ts[p]),
                               jnp.where(head0, r_ts[p], 0.0), jnp.where(head0, 0.0, r_ts[p])], axis=0)
        rhs = jnp.concatenate([pair(b_all, p), pair(k_all, p)], axis=0)
        ms.append(_mm(lhs, rhs, NT, na=n_scores, nb=n_scores))
    yield
    states = [state_ref[p] for p in pairs]
    state_ts = [_bf16_terms(s, n_apply) for s in states]
    v_ts = [_bf16_terms(pair(v, p), n_apply) for p in pairs]
    x0_ts = [_bf16_terms(
        _mm(a_ts[p], state_ts[p], NT, na=n_apply) + jnp.where(
            head0,
            _mm(jnp.where(strict, ms[p][0:c, c:], 0.0), v_ts[p], na=n_apply),
            _mm(jnp.where(strict, ms[p][c:2 * c, c:], 0.0), v_ts[p], na=n_apply)), n_apply)
        for p in pairs]
    yield
    invs = yield from _unit_lower_inverses([jnp.where(strict, ms[p][h * c:(h + 1) * c, :c], 0.0)
                                            for p in pairs for h in range(2)])
    yield
    us =[jnp.where(head0, _mm(invs[2 * p], x0_ts[p], na=n_apply), _mm(invs[2 * p + 1], x0_ts[p], na=n_apply))
          for p in pairs]
    yield
    uv_ts = [_bf16_terms(jnp.concatenate([us[p], pair(v, p)], axis=0), n_apply) for p in pairs]
    ys = [_mm(r_ts[p], state_ts[p], NT, na=n_apply) + jnp.where(
        head0,
        _mm(jnp.where(lower_2, ms[p][2 * c:3 * c, :], 0.0), uv_ts[p], na=n_apply),
        _mm(jnp.where(lower_2, ms[p][3 * c:, :], 0.0), uv_ts[p], na=n_apply)) for p in pairs]
    for p in pairs:
        decayed = jnp.concatenate([pair(bw_all, p), pair(kw_all, p)], axis=0)
        state_ref[p] = states[p] * pair(w_all, p) + ones_h * _mm(uv_ts[p], decayed, TN, nb=n_apply)

    y = jnp.concatenate(ys, axis=1)
    mean = head_sums(y) * (1.0 / HEAD_DIM)
    yc = y - mean
    var = head_sums(yc * yc) * (1.0 / HEAD_DIM)
    yn = yc * lax.rsqrt(var + GN_EPS) * gng_ref[...] + gnb_ref[...]
    bonus = head_sums(r * k * rk_ref[...]) * v
    o_ref[0] = ((yn + bonus) * _silu(gate)).astype(o_ref.dtype)


N_RWKV_INPUTS = 19


def _mixers_kernel(*refs, blk):
    rwkv_in = refs[:N_RWKV_INPUTS]
    ya_ref, yc_ref, state_ref, prev_ref = refs[-4:]
    qn_ref, kn_ref, vb_ref = refs[N_RWKV_INPUTS:N_RWKV_INPUTS + 3]
    gate_refs = refs[N_RWKV_INPUTS + 3:-6]
    suffix_ref, future_ref = refs[-6:-4]
    t = pl.program_id(1)

    @pl.when(t == 0)
    def _reset():
        state_ref[...] = jnp.zeros_like(state_ref)
        prev_ref[...] = jnp.zeros_like(prev_ref)

    attention = _sb_phases(qn_ref, kn_ref, vb_ref, gate_refs, suffix_ref, future_ref, yc_ref,
                           i=t % (kn_ref.shape[1] // blk), blk=blk)
    rwkv = _rwkv_phases(*rwkv_in, ya_ref, state_ref, prev_ref)
    while next(attention) is not SB_TAIL:
        next(rwkv, None)
    for _ in rwkv:
        pass
    next(attention, None)


def _mixers(proj, mu_a, w_up, w0, a_up, a0, k_k, k_a, r_k, gn_g, gn_b, qn_g, kn_g):
    b, s, _ = proj.shape
    chunk, blk = RWKV_CHUNK, SB_BLOCK
    nq = s // blk
    qn, kn, vb = _sb_prep(proj, qn_g, kn_g)

    zeros = jnp.zeros((LORA_RANK, RWKV_WIDTH), F32)
    w_up_pad = jnp.concatenate([w_up, zeros], axis=0)
    a_up_pad = jnp.concatenate([zeros, a_up], axis=0)
    mu = mu_a.reshape(1, A_COLS)
    row = lambda x: x.reshape(1, RWKV_WIDTH)

    def act(section):
        return pl.BlockSpec((1, chunk, RWKV_WIDTH), lambda bi, t: (bi, t, section))

    def mu_spec(section):
        return pl.BlockSpec((1, RWKV_WIDTH), lambda bi, t: (0, section))

    lora_spec = pl.BlockSpec((1, chunk, PAIR), lambda bi, t: (bi, t, LORA_BLK))
    mu_lora_spec = pl.BlockSpec((1, PAIR), lambda bi, t: (0, LORA_BLK))
    up_spec = pl.BlockSpec((PAIR, RWKV_WIDTH), lambda bi, t: (0, 0))
    vec = pl.BlockSpec((1, RWKV_WIDTH), lambda bi, t: (0, 0))
    rwkv_specs = ([act(0), act(1), act(2), act(3), lora_spec,
                   mu_spec(0), mu_spec(1), mu_spec(2), mu_spec(3), mu_lora_spec, up_spec, up_spec] + [vec] * 7)
    rwkv_args = (proj, proj, proj, proj, proj, mu, mu, mu, mu, mu, w_up_pad, a_up_pad,
                 row(w0), row(a0), row(k_k), row(k_a), row(r_k), row(gn_g), row(gn_b))
    assert len(rwkv_specs) == len(rwkv_args) == N_RWKV_INPUTS

    width = SB_PAIRS * PAIR
    gate_blk = C_BLK + 3 * N_PAIRS
    n_terms = SB_SUFFIX_TERMS
    r_i = lax.broadcasted_iota(jnp.int32, (n_terms * blk, blk), 0) % blk
    c_i = lax.broadcasted_iota(jnp.int32, (n_terms * blk, blk), 1)
    suffix = (r_i >= c_i).astype(BF16)
    q_i = lax.broadcasted_iota(jnp.int32, (2 * blk, blk), 0) % blk
    s_i = lax.broadcasted_iota(jnp.int32, (2 * blk, blk), 1)
    future = jnp.where(s_i < q_i, 0.0, SB_MASKED).astype(F32)
    queries = pl.BlockSpec((1, blk, width), lambda bi, t: (bi, t % nq, t // nq))
    keys = pl.BlockSpec((1, s, width), lambda bi, t: (bi, 0, t // nq))
    gates = [pl.BlockSpec((1, blk, PAIR), lambda bi, t, k=k: (bi, t % nq, gate_blk + (t // nq) * SB_PAIRS + k))
             for k in range(SB_PAIRS)]
    sb_specs = [queries, keys, keys] + gates + [
        pl.BlockSpec((n_terms * blk, blk), lambda bi, t: (0, 0)),
        pl.BlockSpec((2 * blk, blk), lambda bi, t: (0, 0))]
    sb_args = (qn, kn, vb) + (proj,) * SB_PAIRS + (suffix, future)

    shape = jax.ShapeDtypeStruct((b, s, RWKV_WIDTH), BF16)
    return pl.pallas_call(
        functools.partial(_mixers_kernel, blk=blk),
        grid=(b, s // chunk),
        in_specs=rwkv_specs + sb_specs,
        out_specs=[pl.BlockSpec((1, chunk, RWKV_WIDTH), lambda bi, t: (bi, t, 0)), queries],
        out_shape=[shape, jax.ShapeDtypeStruct((b, s, SB_WIDTH), BF16)],
        scratch_shapes=[pltpu.VMEM((N_PAIRS, PAIR, PAIR), F32), pltpu.VMEM((8, RWKV_WIDTH), F32)],
        compiler_params=_params(("parallel", "arbitrary")),
        name="mixers",
    )(*rwkv_args, *sb_args)


def kernel(x, norm_g, w_in, mu_a, w_up, w0, a_up, a0, k_k, k_a, r_k, gn_g, gn_b, pool_w, pool_scale,
           qn_g, kn_g, w_out):
    b, s, d = x.shape
    depth = w_in.shape[0]
    x2d = x.reshape(b * s, d)
    for l in range(depth):
        proj = _in_proj(x2d, norm_g[l].reshape(1, d), w_in[l].astype(BF16)).reshape(b, s, IN_COLS)
        ya, yc = _mixers(proj, mu_a[l], w_up[l], w0[l], a_up[l], a0[l], k_k[l], k_a[l], r_k[l], gn_g[l], gn_b[l],
                         qn_g[l], kn_g[l])
        yb = _pool(proj, pool_w[l], pool_scale[l].reshape(1, POOL_WIDTH))
        w_o = w_out[l].astype(BF16)
        x2d = _out_proj(ya.reshape(b * s, RWKV_WIDTH), yb.reshape(b * s, POOL_WIDTH),
                        yc.reshape(b * s, SB_WIDTH),
                        w_o[:RWKV_WIDTH], w_o[RWKV_WIDTH:RWKV_WIDTH + POOL_WIDTH],
                        w_o[RWKV_WIDTH + POOL_WIDTH:], x2d)
    return x2d.reshape(b, s, d)
```

```python
import functools

import jax
import jax.numpy as jnp
from jax import lax
from jax.experimental import pallas as pl
from jax.experimental.pallas import tpu as pltpu

F32 = jnp.float32
BF16 = jnp.bfloat16

D_MODEL = 2048
HEAD_DIM = 64
PAIR = 2 * HEAD_DIM
RWKV_WIDTH = 768
POOL_WIDTH = 512
SB_WIDTH = 768
LORA_RANK = 64
POOL_WINDOWS = (2, 4, 8, 16)
POOL_GROUP_WIDTH = POOL_WIDTH // len(POOL_WINDOWS)
POOL_HALO = 16
RMS_EPS = 1e-6
GN_EPS = 64e-5
LOG2_E = 1.4426950408889634
A_COLS = 4 * RWKV_WIDTH + 2 * LORA_RANK
B_COLS = 2 * POOL_WIDTH
C_COLS = 4 * SB_WIDTH
IN_COLS = A_COLS + B_COLS + C_COLS
N_PAIRS = RWKV_WIDTH // PAIR

LORA_BLK = 4 * RWKV_WIDTH // PAIR
B_BLK = A_COLS // PAIR
C_BLK = (A_COLS + B_COLS) // PAIR

PROJ_DTYPE = BF16

RWKV_CHUNK = 128
INV_BLOCK = 16
RWKV_TERMS = {
    "lora": 2,
    "sums": 2,
    "scores": 1,
    "inverse": 1,
    "apply": 1,
}

SB_PAIRS = 3
SB_BLOCK = RWKV_CHUNK * (N_PAIRS // SB_PAIRS)
SB_ROW_GROUP = 128
SB_UNDERFLOW_LOG2 = 127.0
SB_MASKED = -1e30
SB_PREP_ROWS = 256
SB_NORM_TERMS = 2
SB_SUFFIX_TERMS = 1
SB_TAIL = "tail"

VMEM_LIMIT = 56 * 1024 * 1024

NN = (((1,), (0,)), ((), ()))
NT = (((1,), (1,)), ((), ()))
TN = (((0,), (0,)), ((), ()))


def _dot16(a, b, dims=NN):
    return lax.dot_general(a.astype(BF16), b.astype(BF16), dims, preferred_element_type=F32)


def _bf16_terms(x, n):
    terms = []
    for _ in range(n - 1):
        t = x.astype(BF16)
        terms.append(t)
        x = x - t.astype(F32)
    terms.append(x.astype(BF16))
    return terms


def _mm(a, b, dims=NN, na=1, nb=1):
    a_terms = a if isinstance(a, list) else _bf16_terms(a, na)
    b_terms = b if isinstance(b, list) else _bf16_terms(b, nb)
    keep = max(len(a_terms), len(b_terms))
    out = None
    for i, at in enumerate(a_terms):
        for j, bt in enumerate(b_terms):
            if i + j < keep:
                term = lax.dot_general(at, bt, dims, preferred_element_type=F32)
                out = term if out is None else out + term
    return out


def _sigmoid(x):
    return 1.0 / (1.0 + jnp.exp(-x))


def _silu(x):
    return x * _sigmoid(x)


def _softplus(x):
    return jnp.maximum(x, 0.0) + jnp.log(1.0 + jnp.exp(-jnp.abs(x)))


def _head_block_ones(width=PAIR):
    r = lax.broadcasted_iota(jnp.int32, (width, width), 0) // HEAD_DIM
    c = lax.broadcasted_iota(jnp.int32, (width, width), 1) // HEAD_DIM
    return (r == c).astype(F32)


def _params(sem):
    return pltpu.CompilerParams(dimension_semantics=sem, vmem_limit_bytes=VMEM_LIMIT)


def _in_proj_kernel(x_ref, g_ref, w_ref, o_ref, h_ref, *, row_chunk):
    @pl.when(pl.program_id(1) == 0)
    def _norm():
        def body(c, carry):
            rows = pl.ds(pl.multiple_of(c * row_chunk, row_chunk), row_chunk)
            x = x_ref[rows, :]
            ms = jnp.mean(x * x, axis=-1, keepdims=True)
            h_ref[rows, :] = (x * lax.rsqrt(ms + RMS_EPS) * g_ref[...]).astype(BF16)
            return carry
        lax.fori_loop(0, x_ref.shape[0] // row_chunk, body, 0)

    o_ref[...] = jnp.dot(h_ref[...], w_ref[...], preferred_element_type=F32).astype(o_ref.dtype)


def _in_proj(x2d, g, w_bf16, layer, *, tm=512, tn=2432):
    t, d = x2d.shape
    n = w_bf16.shape[2]
    return pl.pallas_call(
        functools.partial(_in_proj_kernel, row_chunk=128),
        grid=(t // tm, n // tn),
        in_specs=[
            pl.BlockSpec((tm, d), lambda i, j: (i, 0)),
            pl.BlockSpec((1, d), lambda i, j: (0, 0)),
            pl.BlockSpec((None, d, tn), lambda i, j: (layer, 0, j)),
        ],
        out_specs=pl.BlockSpec((tm, tn), lambda i, j: (i, j)),
        out_shape=jax.ShapeDtypeStruct((t, n), PROJ_DTYPE),
        scratch_shapes=[pltpu.VMEM((tm, d), BF16)],
        compiler_params=_params(("parallel", "arbitrary")),
        name="in_proj",
    )(x2d, g, w_bf16)


def _out_proj_kernel(ya_ref, yb_ref, yc_ref, w_ref, x_ref, o_ref):
    pool_row, sb_row = RWKV_WIDTH, RWKV_WIDTH + POOL_WIDTH
    acc = jnp.dot(ya_ref[...], w_ref[0:pool_row, :], preferred_element_type=F32)
    acc += jnp.dot(yb_ref[...], w_ref[pool_row:sb_row, :], preferred_element_type=F32)
    acc += jnp.dot(yc_ref[...], w_ref[sb_row:, :], preferred_element_type=F32)
    o_ref[...] = x_ref[...] + acc


def _out_proj(ya, yb, yc, w_bf16, layer, x2d, *, tm=512):
    t, d = x2d.shape
    row = lambda i: (i, 0)
    return pl.pallas_call(
        _out_proj_kernel,
        grid=(t // tm,),
        in_specs=[
            pl.BlockSpec((tm, RWKV_WIDTH), row),
            pl.BlockSpec((tm, POOL_WIDTH), row),
            pl.BlockSpec((tm, SB_WIDTH), row),
            pl.BlockSpec((None, w_bf16.shape[1], d), lambda i: (layer, 0, 0)),
            pl.BlockSpec((tm, d), row),
        ],
        out_specs=pl.BlockSpec((tm, d), row),
        out_shape=jax.ShapeDtypeStruct((t, d), F32),
        compiler_params=_params(("parallel",)),
        name="out_proj",
    )(ya, yb, yc, w_bf16, x2d)


def _pool_kernel(*refs, ts):
    n_g = len(POOL_WINDOWS)
    cur_refs = refs[0:n_g]
    halo_refs = refs[n_g:2 * n_g]
    gate_refs = refs[2 * n_g:3 * n_g]
    w_ref, scale_ref, o_ref, ext_ref = refs[3 * n_g:]
    i = pl.program_id(1)
    pos = (i * ts + 1 + lax.broadcasted_iota(jnp.int32, (ts, 1), 0)).astype(F32)
    for gi, win in enumerate(POOL_WINDOWS):
        cur = cur_refs[gi][0].astype(F32)
        ext_ref[0:POOL_HALO, :] = jnp.where(i == 0, 0.0, halo_refs[gi][0].astype(F32))
        ext_ref[POOL_HALO:, :] = cur
        acc = cur
        for lag in range(1, win):
            acc = acc + ext_ref[POOL_HALO - lag:POOL_HALO - lag + ts, :]
        d = acc / jnp.minimum(pos, float(win)) - cur
        y = _dot16(d, w_ref[gi])
        cols = slice(gi * POOL_GROUP_WIDTH, (gi + 1) * POOL_GROUP_WIDTH)
        y = y * scale_ref[:, cols] * _silu(gate_refs[gi][0].astype(F32))
        o_ref[0, :, cols] = y.astype(o_ref.dtype)


def _pool(proj, pool_w, pool_scale, *, ts=1024):
    b, s, _ = proj.shape
    ts = min(ts, s)
    n_g = len(POOL_WINDOWS)
    halo_blocks = ts // POOL_HALO
    cur_specs = [pl.BlockSpec((1, ts, POOL_GROUP_WIDTH), lambda bi, i, g=g: (bi, i, B_BLK + g))
                 for g in range(n_g)]
    halo_specs = [pl.BlockSpec((1, POOL_HALO, POOL_GROUP_WIDTH),
                               lambda bi, i, g=g: (bi, jnp.maximum(i * halo_blocks - 1, 0), B_BLK + g))
                  for g in range(n_g)]
    gate_specs = [pl.BlockSpec((1, ts, POOL_GROUP_WIDTH), lambda bi, i, g=g: (bi, i, B_BLK + n_g + g))
                  for g in range(n_g)]
    return pl.pallas_call(
        functools.partial(_pool_kernel, ts=ts),
        grid=(b, s // ts),
        in_specs=cur_specs + halo_specs + gate_specs + [
            pl.BlockSpec((n_g, POOL_GROUP_WIDTH, POOL_GROUP_WIDTH), lambda bi, i: (0, 0, 0)),
            pl.BlockSpec((1, POOL_WIDTH), lambda bi, i: (0, 0)),
        ],
        out_specs=pl.BlockSpec((1, ts, POOL_WIDTH), lambda bi, i: (bi, i, 0)),
        out_shape=jax.ShapeDtypeStruct((b, s, POOL_WIDTH), BF16),
        scratch_shapes=[pltpu.VMEM((ts + POOL_HALO, POOL_GROUP_WIDTH), F32)],
        compiler_params=_params(("parallel", "arbitrary")),
        name="pool",
    )(*([proj] * (3 * n_g)), pool_w, pool_scale)


def _sb_prep_kernel(q_ref, k_ref, qg_ref, kg_ref, qn_ref, kn_ref):
    ones_h = _head_block_ones().astype(BF16)

    def normed(x, gain):
        x = x.astype(F32)
        ms = _mm(x * x, ones_h, na=SB_NORM_TERMS) * (1.0 / HEAD_DIM)
        return x * lax.rsqrt(ms + RMS_EPS) * gain

    def body(c, carry):
        rows = pl.ds(pl.multiple_of(c * SB_PREP_ROWS, SB_PREP_ROWS), SB_PREP_ROWS)
        for p in range(q_ref.shape[2] // PAIR):
            lanes = slice(p * PAIR, (p + 1) * PAIR)
            qn_ref[0, rows, lanes] = (normed(q_ref[0, rows, lanes], qg_ref[...])
                                      * (HEAD_DIM ** -0.5 * LOG2_E)).astype(BF16)
            kn_ref[0, rows, lanes] = normed(k_ref[0, rows, lanes], kg_ref[...]).astype(BF16)
        return carry

    lax.fori_loop(0, q_ref.shape[1] // SB_PREP_ROWS, body, 0)


def _sb_prep(proj, qn_g, kn_g, *, ts=2048):
    b, s, _ = proj.shape
    ts = min(ts, s)
    width = SB_PAIRS * PAIR
    groups = SB_WIDTH // width
    first = C_BLK * PAIR // width
    qg = jnp.tile(qn_g.reshape(1, HEAD_DIM), (1, 2))
    kg = jnp.tile(kn_g.reshape(1, HEAD_DIM), (1, 2))
    vec = pl.BlockSpec((1, PAIR), lambda bi, i, j: (0, 0))
    out = pl.BlockSpec((1, ts, width), lambda bi, i, j: (bi, i, j))
    shape = jax.ShapeDtypeStruct((b, s, SB_WIDTH), BF16)
    return pl.pallas_call(
        _sb_prep_kernel,
        grid=(b, s // ts, groups),
        in_specs=[pl.BlockSpec((1, ts, width), lambda bi, i, j, sec=sec: (bi, i, first + sec * groups + j))
                  for sec in range(2)] + [vec, vec],
        out_specs=[out, out],
        out_shape=[shape, shape],
        compiler_params=_params(("parallel", "parallel", "parallel")),
        name="sb_prep",
    )(proj, proj, qg, kg)


def _sb_phases(qn_ref, kn_ref, vb_ref, gate_refs, suffix_ref, future_ref, o_ref, *, i, blk):
    pairs = range(len(gate_refs))
    head0 = lax.broadcasted_iota(jnp.int32, (1, PAIR), 1) < HEAD_DIM

    def lanes(p):
        return slice(p * PAIR, (p + 1) * PAIR)

    zero = jnp.zeros((), BF16)
    qqs = [jnp.concatenate([jnp.where(head0, qn_ref[0, :, lanes(p)], zero),
                            jnp.where(head0, zero, qn_ref[0, :, lanes(p)])], axis=0) for p in pairs]

    n_groups = 2 * blk // SB_ROW_GROUP
    groups = range(n_groups)

    def group(x, g):
        return x[g * SB_ROW_GROUP:(g + 1) * SB_ROW_GROUP]

    def walk(blocks, state):
        waves = [(b, p) for b in range(len(blocks)) for p in pairs]
        key_rows = [pl.ds(pl.multiple_of(j * blk, blk), blk) for j, _, _ in blocks]
        zs = {(b, p): [lax.dot_general(group(qqs[p], g), kn_ref[0, key_rows[b], lanes(p)], NT,
                                       preferred_element_type=F32) for g in groups] for b, p in waves}
        yield
        sums = {}
        for b, p in waves:
            if blocks[b][1]:
                zs[b, p] = [z + future_ref[g * SB_ROW_GROUP:(g + 1) * SB_ROW_GROUP, :]
                            for g, z in zip(groups, zs[b, p])]
            drops = [jnp.maximum(z, 0.0) + jnp.log2(1.0 + jnp.exp2(-jnp.abs(z))) for z in zs[b, p]]
            sums[b, p] = [jnp.dot(jnp.concatenate(_bf16_terms(x, SB_SUFFIX_TERMS), axis=1), suffix_ref[...],
                                  preferred_element_type=F32) for x in drops]
            yield
        carries, accs = state["carries"], state["accs"]
        for b, p in waves:
            csums = [s + group(carries[p], g) for g, s in zip(groups, sums[b, p])]
            ws = [jnp.exp2(z - c) for z, c in zip(zs[b, p], csums)]
            if blocks[b][2] is not None:
                ws = [jnp.where(blocks[b][2], w, 0.0) for w in ws]
            vb = vb_ref[0, key_rows[b], lanes(p)].astype(BF16)
            pvs = [jnp.dot(w.astype(BF16), vb, preferred_element_type=F32) for w in ws]
            pv0 = jnp.concatenate(pvs[:n_groups // 2], axis=0)
            pv1 = jnp.concatenate(pvs[n_groups // 2:], axis=0)
            accs[p] = accs[p] + jnp.where(head0, pv0, pv1)
            carries[p] = jnp.concatenate([c[:, 0:1] for c in csums], axis=0)
            yield

    def live(carries):
        lowest = carries[0]
        for c in carries[1:]:
            lowest = jnp.minimum(lowest, c)
        return jnp.min(lowest) < SB_UNDERFLOW_LOG2

    state = {"carries": [jnp.zeros((2 * blk, 1), F32) for _ in pairs],
             "accs": [jnp.zeros((blk, PAIR), F32) for _ in pairs]}
    yield from walk([(i, True, None), (jnp.maximum(i - 1, 0), False, i > 0)], state)
    yield SB_TAIL

    def more(loop):
        return jnp.logical_and(loop[0] < i, loop[1])

    def body(loop):
        n, _, carries, accs = loop
        state = {"carries": list(carries), "accs": list(accs)}
        for _ in walk([(i - 1 - n, False, None)], state):
            pass
        return n + 1, live(state["carries"]), tuple(state["carries"]), tuple(state["accs"])

    _, _, _, accs = lax.while_loop(
        more, body, (jnp.int32(1), live(state["carries"]), tuple(state["carries"]), tuple(state["accs"])))
    for p in pairs:
        o_ref[0, :, lanes(p)] = (accs[p] * _silu(gate_refs[p][0].astype(F32))).astype(o_ref.dtype)


def _unit_lower_inverses(lows):
    c = lows[0].shape[0]
    r_i = lax.broadcasted_iota(jnp.int32, (c, c), 0)
    c_i = lax.broadcasted_iota(jnp.int32, (c, c), 1)
    eye = (r_i == c_i).astype(F32)
    same_block = (r_i // INV_BLOCK) == (c_i // INV_BLOCK)
    diags = [jnp.where(same_block, low, 0.0) for low in lows]
    offs = [jnp.where(same_block, 0.0, low) for low in lows]

    n_inv = RWKV_TERMS["inverse"]

    def neumann(ns, order):
        invs = [eye + n for n in ns]
        powers = [_mm(t, t) for t in (_bf16_terms(n, n_inv) for n in ns)]
        yield
        span = 2
        while 2 * span < order:
            stacked = [_mm(jnp.concatenate([inv, pw], axis=0), pw, na=n_inv, nb=n_inv)
                       for inv, pw in zip(invs, powers)]
            invs = [inv + st[:c] for inv, st in zip(invs, stacked)]
            powers = [st[c:] for st in stacked]
            span *= 2
            yield
        return [inv + _mm(inv, pw, na=n_inv, nb=n_inv) for inv, pw in zip(invs, powers)]

    inv_diags = yield from neumann(diags, INV_BLOCK)
    inv_diags = [_bf16_terms(inv, n_inv) for inv in inv_diags]
    yield
    remainders = [_mm(inv, off, nb=n_inv) for inv, off in zip(inv_diags, offs)]
    yield
    inv_offs = yield from neumann(remainders, c // INV_BLOCK)
    yield
    return [_mm(inv_off, inv_diag, na=n_inv) for inv_off, inv_diag in zip(inv_offs, inv_diags)]


def _rwkv_phases(r_ref, k_ref, v_ref, g_ref, lora_ref,
                 mu_r_ref, mu_k_ref, mu_v_ref, mu_g_ref, mu_l_ref,
                 w_up_ref, a_up_ref, w0_ref, a0_ref, kk_ref, ka_ref, rk_ref, gng_ref, gnb_ref,
                 o_ref, state_ref, prev_ref):
    c = r_ref.shape[1]
    n_pairs = r_ref.shape[2] // PAIR
    n_lora, n_sums, n_scores, n_apply = (RWKV_TERMS[s] for s in ("lora", "sums", "scores", "apply"))

    row = lax.broadcasted_iota(jnp.int32, (c, 1), 0)
    head0 = lax.broadcasted_iota(jnp.int32, (1, PAIR), 1) < HEAD_DIM
    ones_h = _head_block_ones()
    ones_h16 = ones_h.astype(BF16)
    ones_2 = _head_block_ones(2 * PAIR).astype(BF16)

    def pair(x, p):
        return x[:, p * PAIR:(p + 1) * PAIR]

    def head_sums(x):
        parts = [_mm(x[:, p * PAIR:(p + 2) * PAIR], ones_2, na=n_sums) for p in range(0, n_pairs - 1, 2)]
        if n_pairs % 2:
            parts.append(_mm(pair(x, n_pairs - 1), ones_h16, na=n_sums))
        return jnp.concatenate(parts, axis=1)

    def shifted(ref, mu_ref, slot):
        x = ref[0].astype(F32)
        width = x.shape[1]
        prev = jnp.where(row == 0, prev_ref[slot:slot + 1, :width], pltpu.roll(x, 1, 0))
        prev_ref[slot:slot + 1, :width] = x[c - 1:c, :]
        return x + (prev - x) * mu_ref[...]

    r = shifted(r_ref, mu_r_ref, 0)
    k = shifted(k_ref, mu_k_ref, 1)
    v = shifted(v_ref, mu_v_ref, 2)
    gate = shifted(g_ref, mu_g_ref, 3)
    lora = shifted(lora_ref, mu_l_ref, 4)

    log_w = -_softplus(-(w0_ref[...] + _mm(jnp.tanh(lora), w_up_ref[...], na=n_lora, nb=n_lora))) - 0.5
    log_decay = -jnp.exp(log_w)
    iclr = _sigmoid(a0_ref[...] + _mm(lora, a_up_ref[...], na=n_lora, nb=n_lora))
    kk = k * kk_ref[...]
    kk = kk / jnp.maximum(jnp.sqrt(head_sums(kk * kk)), 1e-12)
    k = k * (1.0 + (iclr - 1.0) * ka_ref[...])

    r_i = lax.broadcasted_iota(jnp.int32, (c, c), 0)
    c_i = lax.broadcasted_iota(jnp.int32, (c, c), 1)
    lower = r_i >= c_i
    strict = r_i > c_i
    lower_2 = (lax.broadcasted_iota(jnp.int32, (c, 2 * c), 0)
               >= lax.broadcasted_iota(jnp.int32, (c, 2 * c), 1) % c)
    cum = _mm(lower.astype(BF16), log_decay, nb=n_sums)
    total = cum[c - 1:c, :]
    w_incl = jnp.exp(cum)
    w_excl = jnp.exp(cum - log_decay)
    w_inv = jnp.exp(-cum)
    w_rest = jnp.exp(total - cum)
    w_all = jnp.exp(total)

    kk_a = kk * iclr
    a_all = -kk * w_excl
    r_all = r * w_incl
    b_all = kk_a * w_inv
    k_all = k * w_inv
    bw_all = kk_a * w_rest
    kw_all = k * w_rest

    pairs = range(n_pairs)
    a_ts = [pair(a_all, p) for p in pairs]
    r_ts = [pair(r_all, p) for p in pairs]
    ms = []
    for p in pairs:
        lhs = jnp.concatenate([jnp.where(head0, a_ts[p], 0.0), jnp.where(head0, 0.0, a_ts[p]),
                               jnp.where(head0, r_ts[p], 0.0), jnp.where(head0, 0.0, r_ts[p])], axis=0)
        rhs = jnp.concatenate([pair(b_all, p), pair(k_all, p)], axis=0)
        ms.append(_mm(lhs, rhs, NT, na=n_scores, nb=n_scores))
    yield
    states = [state_ref[p] for p in pairs]
    state_ts = [_bf16_terms(s, n_apply) for s in states]
    v_ts = [_bf16_terms(pair(v, p), n_apply) for p in pairs]
    x0_ts = [_bf16_terms(
        _mm(a_ts[p], state_ts[p], NT, na=n_apply) + jnp.where(
            head0,
            _mm(jnp.where(strict, ms[p][0:c, c:], 0.0), v_ts[p], na=n_apply),
            _mm(jnp.where(strict, ms[p][c:2 * c, c:], 0.0), v_ts[p], na=n_apply)), n_apply)
        for p in pairs]
    yield
    invs = yield from _unit_lower_inverses([jnp.where(strict, ms[p][h * c:(h + 1) * c, :c], 0.0)
                                            for p in pairs for h in range(2)])
    yield
    us =[jnp.where(head0, _mm(invs[2 * p], x0_ts[p], na=n_apply), _mm(invs[2 * p + 1], x0_ts[p], na=n_apply))
          for p in pairs]
    yield
    uv_ts = [_bf16_terms(jnp.concatenate([us[p], pair(v, p)], axis=0), n_apply) for p in pairs]
    ys = [_mm(r_ts[p], state_ts[p], NT, na=n_apply) + jnp.where(
        head0,
        _mm(jnp.where(lower_2, ms[p][2 * c:3 * c, :], 0.0), uv_ts[p], na=n_apply),
        _mm(jnp.where(lower_2, ms[p][3 * c:, :], 0.0), uv_ts[p], na=n_apply)) for p in pairs]
    for p in pairs:
        decayed = jnp.concatenate([pair(bw_all, p), pair(kw_all, p)], axis=0)
        state_ref[p] = states[p] * pair(w_all, p) + ones_h * _mm(uv_ts[p], decayed, TN, nb=n_apply)

    y = jnp.concatenate(ys, axis=1)
    mean = head_sums(y) * (1.0 / HEAD_DIM)
    yc = y - mean
    var = head_sums(yc * yc) * (1.0 / HEAD_DIM)
    yn = yc * lax.rsqrt(var + GN_EPS) * gng_ref[...] + gnb_ref[...]
    bonus = head_sums(r * k * rk_ref[...]) * v
    o_ref[0] = ((yn + bonus) * _silu(gate)).astype(o_ref.dtype)


N_RWKV_INPUTS = 19


def _mixers_kernel(*refs, blk):
    rwkv_in = refs[:N_RWKV_INPUTS]
    ya_ref, yc_ref, state_ref, prev_ref = refs[-4:]
    qn_ref, kn_ref, vb_ref = refs[N_RWKV_INPUTS:N_RWKV_INPUTS + 3]
    gate_refs = refs[N_RWKV_INPUTS + 3:-6]
    suffix_ref, future_ref = refs[-6:-4]
    t = pl.program_id(1)

    @pl.when(t == 0)
    def _reset():
        state_ref[...] = jnp.zeros_like(state_ref)
        prev_ref[...] = jnp.zeros_like(prev_ref)

    attention = _sb_phases(qn_ref, kn_ref, vb_ref, gate_refs, suffix_ref, future_ref, yc_ref,
                           i=t % (kn_ref.shape[1] // blk), blk=blk)
    rwkv = _rwkv_phases(*rwkv_in, ya_ref, state_ref, prev_ref)
    while next(attention) is not SB_TAIL:
        next(rwkv, None)
    for _ in rwkv:
        pass
    next(attention, None)


def _mixers(proj, mu_a, w_up, w0, a_up, a0, k_k, k_a, r_k, gn_g, gn_b, qn_g, kn_g):
    b, s, _ = proj.shape
    chunk, blk = RWKV_CHUNK, SB_BLOCK
    nq = s // blk
    qn, kn = _sb_prep(proj, qn_g, kn_g)

    zeros = jnp.zeros((LORA_RANK, RWKV_WIDTH), F32)
    w_up_pad = jnp.concatenate([w_up, zeros], axis=0)
    a_up_pad = jnp.concatenate([zeros, a_up], axis=0)
    mu = mu_a.reshape(1, A_COLS)
    row = lambda x: x.reshape(1, RWKV_WIDTH)

    def act(section):
        return pl.BlockSpec((1, chunk, RWKV_WIDTH), lambda bi, t: (bi, t, section))

    def mu_spec(section):
        return pl.BlockSpec((1, RWKV_WIDTH), lambda bi, t: (0, section))

    lora_spec = pl.BlockSpec((1, chunk, PAIR), lambda bi, t: (bi, t, LORA_BLK))
    mu_lora_spec = pl.BlockSpec((1, PAIR), lambda bi, t: (0, LORA_BLK))
    up_spec = pl.BlockSpec((PAIR, RWKV_WIDTH), lambda bi, t: (0, 0))
    vec = pl.BlockSpec((1, RWKV_WIDTH), lambda bi, t: (0, 0))
    rwkv_specs = ([act(0), act(1), act(2), act(3), lora_spec,
                   mu_spec(0), mu_spec(1), mu_spec(2), mu_spec(3), mu_lora_spec, up_spec, up_spec] + [vec] * 7)
    rwkv_args = (proj, proj, proj, proj, proj, mu, mu, mu, mu, mu, w_up_pad, a_up_pad,
                 row(w0), row(a0), row(k_k), row(k_a), row(r_k), row(gn_g), row(gn_b))
    assert len(rwkv_specs) == len(rwkv_args) == N_RWKV_INPUTS

    width = SB_PAIRS * PAIR
    gate_blk = C_BLK + 3 * N_PAIRS
    n_terms = SB_SUFFIX_TERMS
    r_i = lax.broadcasted_iota(jnp.int32, (n_terms * blk, blk), 0) % blk
    c_i = lax.broadcasted_iota(jnp.int32, (n_terms * blk, blk), 1)
    suffix = (r_i >= c_i).astype(BF16)
    q_i = lax.broadcasted_iota(jnp.int32, (2 * blk, blk), 0) % blk
    s_i = lax.broadcasted_iota(jnp.int32, (2 * blk, blk), 1)
    future = jnp.where(s_i < q_i, 0.0, SB_MASKED).astype(F32)
    queries = pl.BlockSpec((1, blk, width), lambda bi, t: (bi, t % nq, t // nq))
    keys = pl.BlockSpec((1, s, width), lambda bi, t: (bi, 0, t // nq))
    value_blk = (C_BLK + 2 * N_PAIRS) * PAIR // width
    values = pl.BlockSpec((1, s, width), lambda bi, t: (bi, 0, value_blk + t // nq))
    gates = [pl.BlockSpec((1, blk, PAIR), lambda bi, t, k=k: (bi, t % nq, gate_blk + (t // nq) * SB_PAIRS + k))
             for k in range(SB_PAIRS)]
    sb_specs = [queries, keys, values] + gates + [
        pl.BlockSpec((n_terms * blk, blk), lambda bi, t: (0, 0)),
        pl.BlockSpec((2 * blk, blk), lambda bi, t: (0, 0))]
    sb_args = (qn, kn, proj) + (proj,) * SB_PAIRS + (suffix, future)

    shape = jax.ShapeDtypeStruct((b, s, RWKV_WIDTH), BF16)
    return pl.pallas_call(
        functools.partial(_mixers_kernel, blk=blk),
        grid=(b, s // chunk),
        in_specs=rwkv_specs + sb_specs,
        out_specs=[pl.BlockSpec((1, chunk, RWKV_WIDTH), lambda bi, t: (bi, t, 0)), queries],
        out_shape=[shape, jax.ShapeDtypeStruct((b, s, SB_WIDTH), BF16)],
        scratch_shapes=[pltpu.VMEM((N_PAIRS, PAIR, PAIR), F32), pltpu.VMEM((8, RWKV_WIDTH), F32)],
        compiler_params=_params(("parallel", "arbitrary")),
        name="mixers",
    )(*rwkv_args, *sb_args)


def kernel(x, norm_g, w_in, mu_a, w_up, w0, a_up, a0, k_k, k_a, r_k, gn_g, gn_b, pool_w, pool_scale,
           qn_g, kn_g, w_out):
    b, s, d = x.shape
    depth = w_in.shape[0]
    x2d = x.reshape(b * s, d)
    w_in16 = w_in.astype(BF16)
    w_out16 = w_out.astype(BF16)
    for l in range(depth):
        proj = _in_proj(x2d, norm_g[l].reshape(1, d), w_in16, l).reshape(b, s, IN_COLS)
        ya, yc = _mixers(proj, mu_a[l], w_up[l], w0[l], a_up[l], a0[l], k_k[l], k_a[l], r_k[l], gn_g[l], gn_b[l],
                         qn_g[l], kn_g[l])
        yb = _pool(proj, pool_w[l], pool_scale[l].reshape(1, POOL_WIDTH))
        x2d = _out_proj(ya.reshape(b * s, RWKV_WIDTH), yb.reshape(b * s, POOL_WIDTH),
                        yc.reshape(b * s, SB_WIDTH), w_out16, l, x2d)
    return x2d.reshape(b, s, d)
```

```python
import functools

import jax
import jax.numpy as jnp
from jax import lax
from jax.experimental import pallas as pl
from jax.experimental.pallas import tpu as pltpu

F32 = jnp.float32
BF16 = jnp.bfloat16

D_MODEL = 2048
HEAD_DIM = 64
PAIR = 2 * HEAD_DIM
RWKV_WIDTH = 768
POOL_WIDTH = 512
SB_WIDTH = 768
LORA_RANK = 64
POOL_WINDOWS = (2, 4, 8, 16)
POOL_GROUP_WIDTH = POOL_WIDTH // len(POOL_WINDOWS)
POOL_HALO = 16
RMS_EPS = 1e-6
GN_EPS = 64e-5
LOG2_E = 1.4426950408889634
A_COLS = 4 * RWKV_WIDTH + 2 * LORA_RANK
B_COLS = 2 * POOL_WIDTH
C_COLS = 4 * SB_WIDTH
IN_COLS = A_COLS + B_COLS + C_COLS
N_PAIRS = RWKV_WIDTH // PAIR

LORA_BLK = 4 * RWKV_WIDTH // PAIR
B_BLK = A_COLS // PAIR
C_BLK = (A_COLS + B_COLS) // PAIR

PROJ_DTYPE = BF16

RWKV_CHUNK = 128
INV_BLOCK = 16
RWKV_TERMS = {
    "lora": 2,
    "sums": 2,
    "scores": 1,
    "inverse": 1,
    "apply": 1,
}

SB_PAIRS = 3
SB_BLOCK = RWKV_CHUNK * (N_PAIRS // SB_PAIRS)
SB_ROW_GROUP = 256
SB_UNDERFLOW_LOG2 = 127.0
SB_MASKED = -1e30
SB_PREP_ROWS = 256
SB_NORM_TERMS = 2
SB_SUFFIX_TERMS = 1
RWKV_GROUPS = (range(0, N_PAIRS),)
RWKV_STAGGER = 4
RWKV_LORA_SLOT = 4
MIX_ROWS = 1
SB_TAIL = "tail"

VMEM_LIMIT = 56 * 1024 * 1024

NN = (((1,), (0,)), ((), ()))
NT = (((1,), (1,)), ((), ()))
TN = (((0,), (0,)), ((), ()))


def _dot16(a, b, dims=NN):
    return lax.dot_general(a.astype(BF16), b.astype(BF16), dims, preferred_element_type=F32)


def _bf16_terms(x, n):
    terms = []
    for _ in range(n - 1):
        t = x.astype(BF16)
        terms.append(t)
        x = x - t.astype(F32)
    terms.append(x.astype(BF16))
    return terms


def _mm(a, b, dims=NN, na=1, nb=1):
    a_terms = a if isinstance(a, list) else _bf16_terms(a, na)
    b_terms = b if isinstance(b, list) else _bf16_terms(b, nb)
    keep = max(len(a_terms), len(b_terms))
    out = None
    for i, at in enumerate(a_terms):
        for j, bt in enumerate(b_terms):
            if i + j < keep:
                term = lax.dot_general(at, bt, dims, preferred_element_type=F32)
                out = term if out is None else out + term
    return out


def _sigmoid(x):
    return 1.0 / (1.0 + jnp.exp(-x))


def _silu(x):
    return x * _sigmoid(x)


def _softplus(x):
    return jnp.maximum(x, 0.0) + jnp.log(1.0 + jnp.exp(-jnp.abs(x)))


def _head_block_ones(width=PAIR):
    r = lax.broadcasted_iota(jnp.int32, (width, width), 0) // HEAD_DIM
    c = lax.broadcasted_iota(jnp.int32, (width, width), 1) // HEAD_DIM
    return (r == c).astype(F32)


def _params(sem):
    return pltpu.CompilerParams(dimension_semantics=sem, vmem_limit_bytes=VMEM_LIMIT)


def _in_proj_kernel(x_ref, g_ref, w_ref, o_ref, h_ref, *, row_chunk):
    @pl.when(pl.program_id(1) == 0)
    def _norm():
        def body(c, carry):
            rows = pl.ds(pl.multiple_of(c * row_chunk, row_chunk), row_chunk)
            x = x_ref[rows, :]
            ms = jnp.mean(x * x, axis=-1, keepdims=True)
            h_ref[rows, :] = (x * lax.rsqrt(ms + RMS_EPS) * g_ref[...]).astype(BF16)
            return carry
        lax.fori_loop(0, x_ref.shape[0] // row_chunk, body, 0)

    o_ref[...] = jnp.dot(h_ref[...], w_ref[...], preferred_element_type=F32).astype(o_ref.dtype)


def _in_proj(x2d, g, w_bf16, layer, *, tm=1024, tn=2432):
    t, d = x2d.shape
    n = w_bf16.shape[2]
    return pl.pallas_call(
        functools.partial(_in_proj_kernel, row_chunk=128),
        grid=(t // tm, n // tn),
        in_specs=[
            pl.BlockSpec((tm, d), lambda i, j: (i, 0)),
            pl.BlockSpec((1, d), lambda i, j: (0, 0)),
            pl.BlockSpec((None, d, tn), lambda i, j: (layer, 0, j)),
        ],
        out_specs=pl.BlockSpec((tm, tn), lambda i, j: (i, j)),
        out_shape=jax.ShapeDtypeStruct((t, n), PROJ_DTYPE),
        scratch_shapes=[pltpu.VMEM((tm, d), BF16)],
        compiler_params=_params(("parallel", "arbitrary")),
        name="in_proj",
    )(x2d, g, w_bf16)


def _out_proj_kernel(ya_ref, yb_ref, yc_ref, w_ref, x_ref, o_ref):
    pool_row, sb_row = RWKV_WIDTH, RWKV_WIDTH + POOL_WIDTH
    acc = jnp.dot(ya_ref[...], w_ref[0:pool_row, :], preferred_element_type=F32)
    acc += jnp.dot(yb_ref[...], w_ref[pool_row:sb_row, :], preferred_element_type=F32)
    acc += jnp.dot(yc_ref[...], w_ref[sb_row:, :], preferred_element_type=F32)
    o_ref[...] = x_ref[...] + acc


def _out_proj(ya, yb, yc, w_bf16, layer, x2d, *, tm=512):
    t, d = x2d.shape
    row = lambda i: (i, 0)
    return pl.pallas_call(
        _out_proj_kernel,
        grid=(t // tm,),
        in_specs=[
            pl.BlockSpec((tm, RWKV_WIDTH), row),
            pl.BlockSpec((tm, POOL_WIDTH), row),
            pl.BlockSpec((tm, SB_WIDTH), row),
            pl.BlockSpec((None, w_bf16.shape[1], d), lambda i: (layer, 0, 0)),
            pl.BlockSpec((tm, d), row),
        ],
        out_specs=pl.BlockSpec((tm, d), row),
        out_shape=jax.ShapeDtypeStruct((t, d), F32),
        compiler_params=_params(("parallel",)),
        name="out_proj",
    )(ya, yb, yc, w_bf16, x2d)


def _pool_kernel(*refs, ts):
    n_g = len(POOL_WINDOWS)
    cur_refs = refs[0:n_g]
    halo_refs = refs[n_g:2 * n_g]
    gate_refs = refs[2 * n_g:3 * n_g]
    w_ref, scale_ref, o_ref, ext_ref = refs[3 * n_g:]
    i = pl.program_id(1)
    pos = (i * ts + 1 + lax.broadcasted_iota(jnp.int32, (ts, 1), 0)).astype(F32)
    for gi, win in enumerate(POOL_WINDOWS):
        cur = cur_refs[gi][0].astype(F32)
        ext_ref[0:POOL_HALO, :] = jnp.where(i == 0, 0.0, halo_refs[gi][0].astype(F32))
        ext_ref[POOL_HALO:, :] = cur
        acc = cur
        for lag in range(1, win):
            acc = acc + ext_ref[POOL_HALO - lag:POOL_HALO - lag + ts, :]
        d = acc / jnp.minimum(pos, float(win)) - cur
        y = _dot16(d, w_ref[gi])
        cols = slice(gi * POOL_GROUP_WIDTH, (gi + 1) * POOL_GROUP_WIDTH)
        y = y * scale_ref[:, cols] * _silu(gate_refs[gi][0].astype(F32))
        o_ref[0, :, cols] = y.astype(o_ref.dtype)


def _pool(proj, pool_w, pool_scale, *, ts=1024):
    b, s, _ = proj.shape
    ts = min(ts, s)
    n_g = len(POOL_WINDOWS)
    halo_blocks = ts // POOL_HALO
    cur_specs = [pl.BlockSpec((1, ts, POOL_GROUP_WIDTH), lambda bi, i, g=g: (bi, i, B_BLK + g))
                 for g in range(n_g)]
    halo_specs = [pl.BlockSpec((1, POOL_HALO, POOL_GROUP_WIDTH),
                               lambda bi, i, g=g: (bi, jnp.maximum(i * halo_blocks - 1, 0), B_BLK + g))
                  for g in range(n_g)]
    gate_specs = [pl.BlockSpec((1, ts, POOL_GROUP_WIDTH), lambda bi, i, g=g: (bi, i, B_BLK + n_g + g))
                  for g in range(n_g)]
    return pl.pallas_call(
        functools.partial(_pool_kernel, ts=ts),
        grid=(b, s // ts),
        in_specs=cur_specs + halo_specs + gate_specs + [
            pl.BlockSpec((n_g, POOL_GROUP_WIDTH, POOL_GROUP_WIDTH), lambda bi, i: (0, 0, 0)),
            pl.BlockSpec((1, POOL_WIDTH), lambda bi, i: (0, 0)),
        ],
        out_specs=pl.BlockSpec((1, ts, POOL_WIDTH), lambda bi, i: (bi, i, 0)),
        out_shape=jax.ShapeDtypeStruct((b, s, POOL_WIDTH), BF16),
        scratch_shapes=[pltpu.VMEM((ts + POOL_HALO, POOL_GROUP_WIDTH), F32)],
        compiler_params=_params(("parallel", "arbitrary")),
        name="pool",
    )(*([proj] * (3 * n_g)), pool_w, pool_scale)


def _sb_prep_kernel(q_ref, k_ref, qg_ref, kg_ref, qn_ref, kn_ref):
    ones_h = _head_block_ones().astype(BF16)

    def normed(x, gain):
        x = x.astype(F32)
        ms = _mm(x * x, ones_h, na=SB_NORM_TERMS) * (1.0 / HEAD_DIM)
        return x * lax.rsqrt(ms + RMS_EPS) * gain

    def body(c, carry):
        rows = pl.ds(pl.multiple_of(c * SB_PREP_ROWS, SB_PREP_ROWS), SB_PREP_ROWS)
        for p in range(q_ref.shape[2] // PAIR):
            lanes = slice(p * PAIR, (p + 1) * PAIR)
            qn_ref[0, rows, lanes] = (normed(q_ref[0, rows, lanes], qg_ref[...])
                                      * (HEAD_DIM ** -0.5 * LOG2_E)).astype(BF16)
            kn_ref[0, rows, lanes] = normed(k_ref[0, rows, lanes], kg_ref[...]).astype(BF16)
        return carry

    lax.fori_loop(0, q_ref.shape[1] // SB_PREP_ROWS, body, 0)


def _sb_prep(proj, qn_g, kn_g, *, ts=2048):
    b, s, _ = proj.shape
    ts = min(ts, s)
    width = SB_PAIRS * PAIR
    groups = SB_WIDTH // width
    first = C_BLK * PAIR // width
    qg = jnp.tile(qn_g.reshape(1, HEAD_DIM), (1, 2))
    kg = jnp.tile(kn_g.reshape(1, HEAD_DIM), (1, 2))
    vec = pl.BlockSpec((1, PAIR), lambda bi, i, j: (0, 0))
    out = pl.BlockSpec((1, ts, width), lambda bi, i, j: (bi, i, j))
    shape = jax.ShapeDtypeStruct((b, s, SB_WIDTH), BF16)
    return pl.pallas_call(
        _sb_prep_kernel,
        grid=(b, s // ts, groups),
        in_specs=[pl.BlockSpec((1, ts, width), lambda bi, i, j, sec=sec: (bi, i, first + sec * groups + j))
                  for sec in range(2)] + [vec, vec],
        out_specs=[out, out],
        out_shape=[shape, shape],
        compiler_params=_params(("parallel", "parallel", "parallel")),
        name="sb_prep",
    )(proj, proj, qg, kg)


def _sb_phases(qn_ref, kn_ref, vb_ref, gate_refs, suffix_ref, future_ref, o_ref, *, i, blk):
    pairs = range(len(gate_refs))
    head0 = lax.broadcasted_iota(jnp.int32, (1, PAIR), 1) < HEAD_DIM

    def lanes(p):
        return slice(p * PAIR, (p + 1) * PAIR)

    zero = jnp.zeros((), BF16)
    qqs = [jnp.concatenate([jnp.where(head0, qn_ref[0, :, lanes(p)], zero),
                            jnp.where(head0, zero, qn_ref[0, :, lanes(p)])], axis=0) for p in pairs]

    n_groups = 2 * blk // SB_ROW_GROUP
    groups = range(n_groups)

    def group(x, g):
        return x[g * SB_ROW_GROUP:(g + 1) * SB_ROW_GROUP]

    def walk(blocks, state):
        waves = [(b, p) for b in range(len(blocks)) for p in pairs]
        key_rows = [pl.ds(pl.multiple_of(j * blk, blk), blk) for j, _, _ in blocks]
        zs = {(b, p): [lax.dot_general(group(qqs[p], g), kn_ref[0, key_rows[b], lanes(p)], NT,
                                       preferred_element_type=F32) for g in groups] for b, p in waves}
        yield
        sums = {}
        for b, p in waves:
            if blocks[b][1]:
                zs[b, p] = [z + future_ref[g * SB_ROW_GROUP:(g + 1) * SB_ROW_GROUP, :]
                            for g, z in zip(groups, zs[b, p])]
            drops = [jnp.maximum(z, 0.0) + jnp.log2(1.0 + jnp.exp2(-jnp.abs(z))) for z in zs[b, p]]
            sums[b, p] = [jnp.dot(jnp.concatenate(_bf16_terms(x, SB_SUFFIX_TERMS), axis=1), suffix_ref[...],
                                  preferred_element_type=F32) for x in drops]
            yield
        carries, accs = state["carries"], state["accs"]
        for b, p in waves:
            csums = [s + group(carries[p], g) for g, s in zip(groups, sums[b, p])]
            ws = [jnp.exp2(z - c) for z, c in zip(zs[b, p], csums)]
            if blocks[b][2] is not None:
                ws = [jnp.where(blocks[b][2], w, 0.0) for w in ws]
            vb = vb_ref[0, key_rows[b], lanes(p)].astype(BF16)
            pvs = [jnp.dot(w.astype(BF16), vb, preferred_element_type=F32) for w in ws]
            pv0 = jnp.concatenate(pvs[:n_groups // 2], axis=0)
            pv1 = jnp.concatenate(pvs[n_groups // 2:], axis=0)
            accs[p] = accs[p] + jnp.where(head0, pv0, pv1)
            carries[p] = jnp.concatenate([c[:, 0:1] for c in csums], axis=0)
            yield

    def live(carries):
        lowest = carries[0]
        for c in carries[1:]:
            lowest = jnp.minimum(lowest, c)
        return jnp.min(lowest) < SB_UNDERFLOW_LOG2

    state = {"carries": [jnp.zeros((2 * blk, 1), F32) for _ in pairs],
             "accs": [jnp.zeros((blk, PAIR), F32) for _ in pairs]}
    yield from walk([(i, True, None), (jnp.maximum(i - 1, 0), False, i > 0)], state)
    yield SB_TAIL

    def more(loop):
        return jnp.logical_and(loop[0] < i, loop[1])

    def body(loop):
        n, _, carries, accs = loop
        state = {"carries": list(carries), "accs": list(accs)}
        for _ in walk([(i - 1 - n, False, None)], state):
            pass
        return n + 1, live(state["carries"]), tuple(state["carries"]), tuple(state["accs"])

    _, _, _, accs = lax.while_loop(
        more, body, (jnp.int32(1), live(state["carries"]), tuple(state["carries"]), tuple(state["accs"])))
    for p in pairs:
        o_ref[0, :, lanes(p)] = (accs[p] * _silu(gate_refs[p][0].astype(F32))).astype(o_ref.dtype)


def _unit_lower_inverses(lows):
    c = lows[0].shape[0]
    r_i = lax.broadcasted_iota(jnp.int32, (c, c), 0)
    c_i = lax.broadcasted_iota(jnp.int32, (c, c), 1)
    eye = (r_i == c_i).astype(F32)
    same_block = (r_i // INV_BLOCK) == (c_i // INV_BLOCK)
    diags = [jnp.where(same_block, low, 0.0) for low in lows]
    offs = [jnp.where(same_block, 0.0, low) for low in lows]

    n_inv = RWKV_TERMS["inverse"]

    def neumann(ns, order):
        invs = [eye + n for n in ns]
        powers = [_mm(t, t) for t in (_bf16_terms(n, n_inv) for n in ns)]
        yield
        span = 2
        while 2 * span < order:
            stacked = [_mm(jnp.concatenate([inv, pw], axis=0), pw, na=n_inv, nb=n_inv)
                       for inv, pw in zip(invs, powers)]
            invs = [inv + st[:c] for inv, st in zip(invs, stacked)]
            powers = [st[c:] for st in stacked]
            span *= 2
            yield
        return [inv + _mm(inv, pw, na=n_inv, nb=n_inv) for inv, pw in zip(invs, powers)]

    inv_diags = yield from neumann(diags, INV_BLOCK)
    inv_diags = [_bf16_terms(inv, n_inv) for inv in inv_diags]
    yield
    remainders = [_mm(inv, off, nb=n_inv) for inv, off in zip(inv_diags, offs)]
    yield
    inv_offs = yield from neumann(remainders, c // INV_BLOCK)
    yield
    return [_mm(inv_off, inv_diag, na=n_inv) for inv_off, inv_diag in zip(inv_offs, inv_diags)]


def _rwkv_phases(r_ref, k_ref, v_ref, g_ref, lora_ref,
                 mu_r_ref, mu_k_ref, mu_v_ref, mu_g_ref, mu_l_ref,
                 w_up_ref, a_up_ref, w0_ref, a0_ref, kk_ref, ka_ref, rk_ref, gng_ref, gnb_ref,
                 o_ref, state_ref, prev_ref, *, group, lora_slot):
    c = r_ref.shape[1]
    n_pairs = len(group)
    lanes = slice(group[0] * PAIR, (group[-1] + 1) * PAIR)
    n_lora, n_sums, n_scores, n_apply = (RWKV_TERMS[s] for s in ("lora", "sums", "scores", "apply"))

    row = lax.broadcasted_iota(jnp.int32, (c, 1), 0)
    head0 = lax.broadcasted_iota(jnp.int32, (1, PAIR), 1) < HEAD_DIM
    ones_h = _head_block_ones()
    ones_h16 = ones_h.astype(BF16)
    ones_2 = _head_block_ones(2 * PAIR).astype(BF16)

    def pair(x, p):
        return x[:, p * PAIR:(p + 1) * PAIR]

    def head_sums(x):
        parts = [_mm(x[:, p * PAIR:(p + 2) * PAIR], ones_2, na=n_sums) for p in range(0, n_pairs - 1, 2)]
        if n_pairs % 2:
            parts.append(_mm(pair(x, n_pairs - 1), ones_h16, na=n_sums))
        return jnp.concatenate(parts, axis=1)

    def shifted(ref, mu_ref, slot, cols):
        x = ref[0, :, cols].astype(F32)
        prev = jnp.where(row == 0, prev_ref[slot:slot + 1, cols], pltpu.roll(x, 1, 0))
        prev_ref[slot:slot + 1, cols] = x[c - 1:c, :]
        return x + (prev - x) * mu_ref[:, cols]

    r = shifted(r_ref, mu_r_ref, 0, lanes)
    k = shifted(k_ref, mu_k_ref, 1, lanes)
    v = shifted(v_ref, mu_v_ref, 2, lanes)
    gate = shifted(g_ref, mu_g_ref, 3, lanes)
    lora = shifted(lora_ref, mu_l_ref, lora_slot, slice(0, PAIR))

    log_w = -_softplus(-(w0_ref[:, lanes]
                         + _mm(jnp.tanh(lora), w_up_ref[:, lanes], na=n_lora, nb=n_lora))) - 0.5
    log_decay = -jnp.exp(log_w)
    iclr = _sigmoid(a0_ref[:, lanes] + _mm(lora, a_up_ref[:, lanes], na=n_lora, nb=n_lora))
    kk = k * kk_ref[:, lanes]
    kk = kk / jnp.maximum(jnp.sqrt(head_sums(kk * kk)), 1e-12)
    k = k * (1.0 + (iclr - 1.0) * ka_ref[:, lanes])

    r_i = lax.broadcasted_iota(jnp.int32, (c, c), 0)
    c_i = lax.broadcasted_iota(jnp.int32, (c, c), 1)
    lower = r_i >= c_i
    strict = r_i > c_i
    lower_2 = (lax.broadcasted_iota(jnp.int32, (c, 2 * c), 0)
               >= lax.broadcasted_iota(jnp.int32, (c, 2 * c), 1) % c)
    cum = _mm(lower.astype(BF16), log_decay, nb=n_sums)
    total = cum[c - 1:c, :]
    w_incl = jnp.exp(cum)
    w_excl = jnp.exp(cum - log_decay)
    w_inv = jnp.exp(-cum)
    w_rest = jnp.exp(total - cum)
    w_all = jnp.exp(total)

    kk_a = kk * iclr
    a_all = -kk * w_excl
    r_all = r * w_incl
    b_all = kk_a * w_inv
    k_all = k * w_inv
    bw_all = kk_a * w_rest
    kw_all = k * w_rest

    pairs = range(n_pairs)
    a_ts = [pair(a_all, p) for p in pairs]
    r_ts = [pair(r_all, p) for p in pairs]
    ms = []
    for p in pairs:
        lhs = jnp.concatenate([jnp.where(head0, a_ts[p], 0.0), jnp.where(head0, 0.0, a_ts[p]),
                               jnp.where(head0, r_ts[p], 0.0), jnp.where(head0, 0.0, r_ts[p])], axis=0)
        rhs = jnp.concatenate([pair(b_all, p), pair(k_all, p)], axis=0)
        ms.append(_mm(lhs, rhs, NT, na=n_scores, nb=n_scores))
    yield
    states = [state_ref[group[p]] for p in pairs]
    state_ts = [_bf16_terms(s, n_apply) for s in states]
    v_ts = [_bf16_terms(pair(v, p), n_apply) for p in pairs]
    x0_ts = [_bf16_terms(
        _mm(a_ts[p], state_ts[p], NT, na=n_apply) + jnp.where(
            head0,
            _mm(jnp.where(strict, ms[p][0:c, c:], 0.0), v_ts[p], na=n_apply),
            _mm(jnp.where(strict, ms[p][c:2 * c, c:], 0.0), v_ts[p], na=n_apply)), n_apply)
        for p in pairs]
    yield
    invs = yield from _unit_lower_inverses([jnp.where(strict, ms[p][h * c:(h + 1) * c, :c], 0.0)
                                            for p in pairs for h in range(2)])
    yield
    us =[jnp.where(head0, _mm(invs[2 * p], x0_ts[p], na=n_apply), _mm(invs[2 * p + 1], x0_ts[p], na=n_apply))
          for p in pairs]
    yield
    uv_ts = [_bf16_terms(jnp.concatenate([us[p], pair(v, p)], axis=0), n_apply) for p in pairs]
    ys = [_mm(r_ts[p], state_ts[p], NT, na=n_apply) + jnp.where(
        head0,
        _mm(jnp.where(lower_2, ms[p][2 * c:3 * c, :], 0.0), uv_ts[p], na=n_apply),
        _mm(jnp.where(lower_2, ms[p][3 * c:, :], 0.0), uv_ts[p], na=n_apply)) for p in pairs]
    for p in pairs:
        decayed = jnp.concatenate([pair(bw_all, p), pair(kw_all, p)], axis=0)
        state_ref[group[p]] = states[p] * pair(w_all, p) + ones_h * _mm(uv_ts[p], decayed, TN, nb=n_apply)
    yield

    y = jnp.concatenate(ys, axis=1)
    mean = head_sums(y) * (1.0 / HEAD_DIM)
    yc = y - mean
    var = head_sums(yc * yc) * (1.0 / HEAD_DIM)
    yn = yc * lax.rsqrt(var + GN_EPS) * gng_ref[:, lanes] + gnb_ref[:, lanes]
    bonus = head_sums(r * k * rk_ref[:, lanes]) * v
    o_ref[0, :, lanes] = ((yn + bonus) * _silu(gate)).astype(o_ref.dtype)


N_RWKV_INPUTS = 19
N_RWKV_ACTIVATIONS = 5


def _mixers_kernel(*refs, blk):
    rwkv_in = refs[:N_RWKV_INPUTS]
    ya_ref, yc_ref, state_ref, prev_ref = refs[-4:]
    qn_ref, kn_ref, vb_ref = refs[N_RWKV_INPUTS:N_RWKV_INPUTS + 3]
    gate_refs = refs[N_RWKV_INPUTS + 3:-6]
    suffix_ref, future_ref = refs[-6:-4]
    t = pl.program_id(1)

    @pl.when(t == 0)
    def _reset():
        state_ref[...] = jnp.zeros_like(state_ref)
        prev_ref[...] = jnp.zeros_like(prev_ref)

    attentions, rwkvs = [], []
    for row in range(ya_ref.shape[0]):
        one = lambda ref, row=row: ref.at[pl.ds(row, 1)]
        attentions.append(_sb_phases(one(qn_ref), one(kn_ref), one(vb_ref), [one(g) for g in gate_refs],
                                     suffix_ref, future_ref, one(yc_ref),
                                     i=t % (kn_ref.shape[1] // blk), blk=blk))
        rwkvs.append([_rwkv_phases(*[one(r) for r in rwkv_in[:N_RWKV_ACTIVATIONS]],
                                   *rwkv_in[N_RWKV_ACTIVATIONS:], one(ya_ref),
                                   state_ref.at[row], prev_ref.at[row],
                                   group=group, lora_slot=RWKV_LORA_SLOT + n)
                      for n, group in enumerate(RWKV_GROUPS)])
    rwkvs = [chunk for same_group in zip(*rwkvs) for chunk in same_group]
    rounds = 0
    done = object()
    while [next(a) for a in attentions][0] is not SB_TAIL:
        started = rwkvs[:len(attentions) * (1 + rounds // RWKV_STAGGER)]
        for r in started:
            next(r, None)
        rounds += 1
    while rwkvs:
        rwkvs = [r for r in rwkvs if next(r, done) is not done]
    for a in attentions:
        next(a, None)


def _mixers(proj, mu_a, w_up, w0, a_up, a0, k_k, k_a, r_k, gn_g, gn_b, qn_g, kn_g):
    b, s, _ = proj.shape
    chunk, blk = RWKV_CHUNK, SB_BLOCK
    nq = s // blk
    qn, kn = _sb_prep(proj, qn_g, kn_g)

    zeros = jnp.zeros((LORA_RANK, RWKV_WIDTH), F32)
    w_up_pad = jnp.concatenate([w_up, zeros], axis=0)
    a_up_pad = jnp.concatenate([zeros, a_up], axis=0)
    mu = mu_a.reshape(1, A_COLS)
    row = lambda x: x.reshape(1, RWKV_WIDTH)

    nb = MIX_ROWS

    def act(section):
        return pl.BlockSpec((nb, chunk, RWKV_WIDTH), lambda bi, t: (bi, t, section))

    def mu_spec(section):
        return pl.BlockSpec((1, RWKV_WIDTH), lambda bi, t: (0, section))

    lora_spec = pl.BlockSpec((nb, chunk, PAIR), lambda bi, t: (bi, t, LORA_BLK))
    mu_lora_spec = pl.BlockSpec((1, PAIR), lambda bi, t: (0, LORA_BLK))
    up_spec = pl.BlockSpec((PAIR, RWKV_WIDTH), lambda bi, t: (0, 0))
    vec = pl.BlockSpec((1, RWKV_WIDTH), lambda bi, t: (0, 0))
    rwkv_specs = ([act(0), act(1), act(2), act(3), lora_spec,
                   mu_spec(0), mu_spec(1), mu_spec(2), mu_spec(3), mu_lora_spec, up_spec, up_spec] + [vec] * 7)
    rwkv_args = (proj, proj, proj, proj, proj, mu, mu, mu, mu, mu, w_up_pad, a_up_pad,
                 row(w0), row(a0), row(k_k), row(k_a), row(r_k), row(gn_g), row(gn_b))
    assert len(rwkv_specs) == len(rwkv_args) == N_RWKV_INPUTS

    width = SB_PAIRS * PAIR
    gate_blk = C_BLK + 3 * N_PAIRS
    n_terms = SB_SUFFIX_TERMS
    r_i = lax.broadcasted_iota(jnp.int32, (n_terms * blk, blk), 0) % blk
    c_i = lax.broadcasted_iota(jnp.int32, (n_terms * blk, blk), 1)
    suffix = (r_i >= c_i).astype(BF16)
    q_i = lax.broadcasted_iota(jnp.int32, (2 * blk, blk), 0) % blk
    s_i = lax.broadcasted_iota(jnp.int32, (2 * blk, blk), 1)
    future = jnp.where(s_i < q_i, 0.0, SB_MASKED).astype(F32)
    queries = pl.BlockSpec((nb, blk, width), lambda bi, t: (bi, t % nq, t // nq))
    keys = pl.BlockSpec((nb, s, width), lambda bi, t: (bi, 0, t // nq))
    value_blk = (C_BLK + 2 * N_PAIRS) * PAIR // width
    values = pl.BlockSpec((nb, s, width), lambda bi, t: (bi, 0, value_blk + t // nq))
    gates = [pl.BlockSpec((nb, blk, PAIR), lambda bi, t, k=k: (bi, t % nq, gate_blk + (t // nq) * SB_PAIRS + k))
             for k in range(SB_PAIRS)]
    sb_specs = [queries, keys, values] + gates + [
        pl.BlockSpec((n_terms * blk, blk), lambda bi, t: (0, 0)),
        pl.BlockSpec((2 * blk, blk), lambda bi, t: (0, 0))]
    sb_args = (qn, kn, proj) + (proj,) * SB_PAIRS + (suffix, future)

    shape = jax.ShapeDtypeStruct((b, s, RWKV_WIDTH), BF16)
    return pl.pallas_call(
        functools.partial(_mixers_kernel, blk=blk),
        grid=(b // nb, s // chunk),
        in_specs=rwkv_specs + sb_specs,
        out_specs=[pl.BlockSpec((nb, chunk, RWKV_WIDTH), lambda bi, t: (bi, t, 0)), queries],
        out_shape=[shape, jax.ShapeDtypeStruct((b, s, SB_WIDTH), BF16)],
        scratch_shapes=[pltpu.VMEM((nb, N_PAIRS, PAIR, PAIR), F32), pltpu.VMEM((nb, 8, RWKV_WIDTH), F32)],
        compiler_params=_params(("parallel", "arbitrary")),
        name="mixers",
    )(*rwkv_args, *sb_args)


def kernel(x, norm_g, w_in, mu_a, w_up, w0, a_up, a0, k_k, k_a, r_k, gn_g, gn_b, pool_w, pool_scale,
           qn_g, kn_g, w_out):
    b, s, d = x.shape
    depth = w_in.shape[0]
    x2d = x.reshape(b * s, d)
    w_in16 = w_in.astype(BF16)
    w_out16 = w_out.astype(BF16)
    for l in range(depth):
        proj = _in_proj(x2d, norm_g[l].reshape(1, d), w_in16, l).reshape(b, s, IN_COLS)
        ya, yc = _mixers(proj, mu_a[l], w_up[l], w0[l], a_up[l], a0[l], k_k[l], k_a[l], r_k[l], gn_g[l], gn_b[l],
                         qn_g[l], kn_g[l])
        yb = _pool(proj, pool_w[l], pool_scale[l].reshape(1, POOL_WIDTH))
        x2d = _out_proj(ya.reshape(b * s, RWKV_WIDTH), yb.reshape(b * s, POOL_WIDTH),
                        yc.reshape(b * s, SB_WIDTH), w_out16, l, x2d)
    return x2d.reshape(b, s, d)
```

```python
import functools

import jax
import jax.numpy as jnp
from jax import lax
from jax.experimental import pallas as pl
from jax.experimental.pallas import tpu as pltpu

F32 = jnp.float32
BF16 = jnp.bfloat16

D_MODEL = 2048
HEAD_DIM = 64
PAIR = 2 * HEAD_DIM
RWKV_WIDTH = 768
POOL_WIDTH = 512
SB_WIDTH = 768
LORA_RANK = 64
POOL_WINDOWS = (2, 4, 8, 16)
POOL_GROUP_WIDTH = POOL_WIDTH // len(POOL_WINDOWS)
POOL_HALO = 16
RMS_EPS = 1e-6
GN_EPS = 64e-5
LOG2_E = 1.4426950408889634
A_COLS = 4 * RWKV_WIDTH + 2 * LORA_RANK
B_COLS = 2 * POOL_WIDTH
C_COLS = 4 * SB_WIDTH
IN_COLS = A_COLS + B_COLS + C_COLS
N_PAIRS = RWKV_WIDTH // PAIR

LORA_BLK = 4 * RWKV_WIDTH // PAIR
B_BLK = A_COLS // PAIR
C_BLK = (A_COLS + B_COLS) // PAIR

PROJ_DTYPE = BF16

RWKV_CHUNK = 128
INV_BLOCK = 16
RWKV_TERMS = {
    "lora": 2,
    "sums": 2,
    "scores": 1,
    "inverse": 1,
    "apply": 1,
}

SB_PAIRS = 3
SB_BLOCK = RWKV_CHUNK * (N_PAIRS // SB_PAIRS)
SB_ROW_GROUP = 256
SB_UNDERFLOW_LOG2 = 127.0
SB_MASKED = -1e30
SB_PREP_ROWS = 256
SB_NORM_TERMS = 2
SB_SUFFIX_TERMS = 1
RWKV_GROUPS = (range(0, N_PAIRS),)
RWKV_STAGGER = 4
RWKV_LORA_SLOT = 4
MIX_ROWS = 2
SB_TAIL = "tail"

VMEM_LIMIT = 56 * 1024 * 1024

NN = (((1,), (0,)), ((), ()))
NT = (((1,), (1,)), ((), ()))
TN = (((0,), (0,)), ((), ()))


def _dot16(a, b, dims=NN):
    return lax.dot_general(a.astype(BF16), b.astype(BF16), dims, preferred_element_type=F32)


def _bf16_terms(x, n):
    terms = []
    for _ in range(n - 1):
        t = x.astype(BF16)
        terms.append(t)
        x = x - t.astype(F32)
    terms.append(x.astype(BF16))
    return terms


def _mm(a, b, dims=NN, na=1, nb=1):
    a_terms = a if isinstance(a, list) else _bf16_terms(a, na)
    b_terms = b if isinstance(b, list) else _bf16_terms(b, nb)
    keep = max(len(a_terms), len(b_terms))
    out = None
    for i, at in enumerate(a_terms):
        for j, bt in enumerate(b_terms):
            if i + j < keep:
                term = lax.dot_general(at, bt, dims, preferred_element_type=F32)
                out = term if out is None else out + term
    return out


def _sigmoid(x):
    return 1.0 / (1.0 + jnp.exp(-x))


def _silu(x):
    return x * _sigmoid(x)


def _softplus(x):
    return jnp.maximum(x, 0.0) + jnp.log(1.0 + jnp.exp(-jnp.abs(x)))


def _head_block_ones(width=PAIR):
    r = lax.broadcasted_iota(jnp.int32, (width, width), 0) // HEAD_DIM
    c = lax.broadcasted_iota(jnp.int32, (width, width), 1) // HEAD_DIM
    return (r == c).astype(F32)


def _params(sem):
    return pltpu.CompilerParams(dimension_semantics=sem, vmem_limit_bytes=VMEM_LIMIT)


def _in_proj_kernel(x_ref, g_ref, w_ref, o_ref, h_ref, *, row_chunk):
    @pl.when(pl.program_id(1) == 0)
    def _norm():
        def body(c, carry):
            rows = pl.ds(pl.multiple_of(c * row_chunk, row_chunk), row_chunk)
            x = x_ref[rows, :]
            ms = jnp.mean(x * x, axis=-1, keepdims=True)
            h_ref[rows, :] = (x * lax.rsqrt(ms + RMS_EPS) * g_ref[...]).astype(BF16)
            return carry
        lax.fori_loop(0, x_ref.shape[0] // row_chunk, body, 0)

    o_ref[...] = jnp.dot(h_ref[...], w_ref[...], preferred_element_type=F32).astype(o_ref.dtype)


def _in_proj(x2d, g, w_bf16, layer, *, tm=1024, tn=2432):
    t, d = x2d.shape
    n = w_bf16.shape[2]
    return pl.pallas_call(
        functools.partial(_in_proj_kernel, row_chunk=128),
        grid=(t // tm, n // tn),
        in_specs=[
            pl.BlockSpec((tm, d), lambda i, j: (i, 0)),
            pl.BlockSpec((1, d), lambda i, j: (0, 0)),
            pl.BlockSpec((None, d, tn), lambda i, j: (layer, 0, j)),
        ],
        out_specs=pl.BlockSpec((tm, tn), lambda i, j: (i, j)),
        out_shape=jax.ShapeDtypeStruct((t, n), PROJ_DTYPE),
        scratch_shapes=[pltpu.VMEM((tm, d), BF16)],
        compiler_params=_params(("parallel", "arbitrary")),
        name="in_proj",
    )(x2d, g, w_bf16)


def _out_proj_kernel(ya_ref, yb_ref, yc_ref, w_ref, x_ref, o_ref):
    pool_row, sb_row = RWKV_WIDTH, RWKV_WIDTH + POOL_WIDTH
    acc = jnp.dot(ya_ref[...], w_ref[0:pool_row, :], preferred_element_type=F32)
    acc += jnp.dot(yb_ref[...], w_ref[pool_row:sb_row, :], preferred_element_type=F32)
    acc += jnp.dot(yc_ref[...], w_ref[sb_row:, :], preferred_element_type=F32)
    o_ref[...] = x_ref[...] + acc


def _out_proj(ya, yb, yc, w_bf16, layer, x2d, *, tm=512):
    t, d = x2d.shape
    row = lambda i: (i, 0)
    return pl.pallas_call(
        _out_proj_kernel,
        grid=(t // tm,),
        in_specs=[
            pl.BlockSpec((tm, RWKV_WIDTH), row),
            pl.BlockSpec((tm, POOL_WIDTH), row),
            pl.BlockSpec((tm, SB_WIDTH), row),
            pl.BlockSpec((None, w_bf16.shape[1], d), lambda i: (layer, 0, 0)),
            pl.BlockSpec((tm, d), row),
        ],
        out_specs=pl.BlockSpec((tm, d), row),
        out_shape=jax.ShapeDtypeStruct((t, d), F32),
        compiler_params=_params(("parallel",)),
        name="out_proj",
    )(ya, yb, yc, w_bf16, x2d)


def _pool_kernel(*refs, ts):
    n_g = len(POOL_WINDOWS)
    cur_refs = refs[0:n_g]
    halo_refs = refs[n_g:2 * n_g]
    gate_refs = refs[2 * n_g:3 * n_g]
    w_ref, scale_ref, o_ref, ext_ref = refs[3 * n_g:]
    i = pl.program_id(1)
    pos = (i * ts + 1 + lax.broadcasted_iota(jnp.int32, (ts, 1), 0)).astype(F32)
    for gi, win in enumerate(POOL_WINDOWS):
        cur = cur_refs[gi][0].astype(F32)
        ext_ref[0:POOL_HALO, :] = jnp.where(i == 0, 0.0, halo_refs[gi][0].astype(F32))
        ext_ref[POOL_HALO:, :] = cur
        acc = cur
        for lag in range(1, win):
            acc = acc + ext_ref[POOL_HALO - lag:POOL_HALO - lag + ts, :]
        d = acc / jnp.minimum(pos, float(win)) - cur
        y = _dot16(d, w_ref[gi])
        cols = slice(gi * POOL_GROUP_WIDTH, (gi + 1) * POOL_GROUP_WIDTH)
        y = y * scale_ref[:, cols] * _silu(gate_refs[gi][0].astype(F32))
        o_ref[0, :, cols] = y.astype(o_ref.dtype)


def _pool(proj, pool_w, pool_scale, *, ts=1024):
    b, s, _ = proj.shape
    ts = min(ts, s)
    n_g = len(POOL_WINDOWS)
    halo_blocks = ts // POOL_HALO
    cur_specs = [pl.BlockSpec((1, ts, POOL_GROUP_WIDTH), lambda bi, i, g=g: (bi, i, B_BLK + g))
                 for g in range(n_g)]
    halo_specs = [pl.BlockSpec((1, POOL_HALO, POOL_GROUP_WIDTH),
                               lambda bi, i, g=g: (bi, jnp.maximum(i * halo_blocks - 1, 0), B_BLK + g))
                  for g in range(n_g)]
    gate_specs = [pl.BlockSpec((1, ts, POOL_GROUP_WIDTH), lambda bi, i, g=g: (bi, i, B_BLK + n_g + g))
                  for g in range(n_g)]
    return pl.pallas_call(
        functools.partial(_pool_kernel, ts=ts),
        grid=(b, s // ts),
        in_specs=cur_specs + halo_specs + gate_specs + [
            pl.BlockSpec((n_g, POOL_GROUP_WIDTH, POOL_GROUP_WIDTH), lambda bi, i: (0, 0, 0)),
            pl.BlockSpec((1, POOL_WIDTH), lambda bi, i: (0, 0)),
        ],
        out_specs=pl.BlockSpec((1, ts, POOL_WIDTH), lambda bi, i: (bi, i, 0)),
        out_shape=jax.ShapeDtypeStruct((b, s, POOL_WIDTH), BF16),
        scratch_shapes=[pltpu.VMEM((ts + POOL_HALO, POOL_GROUP_WIDTH), F32)],
        compiler_params=_params(("parallel", "arbitrary")),
        name="pool",
    )(*([proj] * (3 * n_g)), pool_w, pool_scale)


def _sb_prep_kernel(q_ref, k_ref, qg_ref, kg_ref, qn_ref, kn_ref):
    ones_h = _head_block_ones().astype(BF16)

    def normed(x, gain):
        x = x.astype(F32)
        ms = _mm(x * x, ones_h, na=SB_NORM_TERMS) * (1.0 / HEAD_DIM)
        return x * lax.rsqrt(ms + RMS_EPS) * gain

    def body(c, carry):
        rows = pl.ds(pl.multiple_of(c * SB_PREP_ROWS, SB_PREP_ROWS), SB_PREP_ROWS)
        for p in range(q_ref.shape[2] // PAIR):
            lanes = slice(p * PAIR, (p + 1) * PAIR)
            qn_ref[0, rows, lanes] = (normed(q_ref[0, rows, lanes], qg_ref[...])
                                      * (HEAD_DIM ** -0.5 * LOG2_E)).astype(BF16)
            kn_ref[0, rows, lanes] = normed(k_ref[0, rows, lanes], kg_ref[...]).astype(BF16)
        return carry

    lax.fori_loop(0, q_ref.shape[1] // SB_PREP_ROWS, body, 0)


def _sb_prep(proj, qn_g, kn_g, *, ts=2048):
    b, s, _ = proj.shape
    ts = min(ts, s)
    width = SB_PAIRS * PAIR
    groups = SB_WIDTH // width
    first = C_BLK * PAIR // width
    qg = jnp.tile(qn_g.reshape(1, HEAD_DIM), (1, 2))
    kg = jnp.tile(kn_g.reshape(1, HEAD_DIM), (1, 2))
    vec = pl.BlockSpec((1, PAIR), lambda bi, i, j: (0, 0))
    out = pl.BlockSpec((1, ts, width), lambda bi, i, j: (bi, i, j))
    shape = jax.ShapeDtypeStruct((b, s, SB_WIDTH), BF16)
    return pl.pallas_call(
        _sb_prep_kernel,
        grid=(b, s // ts, groups),
        in_specs=[pl.BlockSpec((1, ts, width), lambda bi, i, j, sec=sec: (bi, i, first + sec * groups + j))
                  for sec in range(2)] + [vec, vec],
        out_specs=[out, out],
        out_shape=[shape, shape],
        compiler_params=_params(("parallel", "parallel", "parallel")),
        name="sb_prep",
    )(proj, proj, qg, kg)


def _sb_phases(qn_ref, kn_ref, vb_ref, gate_refs, suffix_ref, future_ref, o_ref, *, i, blk):
    pairs = range(len(gate_refs))
    head0 = lax.broadcasted_iota(jnp.int32, (1, PAIR), 1) < HEAD_DIM

    def lanes(p):
        return slice(p * PAIR, (p + 1) * PAIR)

    zero = jnp.zeros((), BF16)
    qqs = [jnp.concatenate([jnp.where(head0, qn_ref[0, :, lanes(p)], zero),
                            jnp.where(head0, zero, qn_ref[0, :, lanes(p)])], axis=0) for p in pairs]

    n_groups = 2 * blk // SB_ROW_GROUP
    groups = range(n_groups)

    def group(x, g):
        return x[g * SB_ROW_GROUP:(g + 1) * SB_ROW_GROUP]

    def walk(blocks, state):
        waves = [(b, p) for b in range(len(blocks)) for p in pairs]
        key_rows = [pl.ds(pl.multiple_of(j * blk, blk), blk) for j, _, _ in blocks]
        zs = {(b, p): [lax.dot_general(group(qqs[p], g), kn_ref[0, key_rows[b], lanes(p)], NT,
                                       preferred_element_type=F32) for g in groups] for b, p in waves}
        yield
        sums = {}
        for b, p in waves:
            if blocks[b][1]:
                zs[b, p] = [z + future_ref[g * SB_ROW_GROUP:(g + 1) * SB_ROW_GROUP, :]
                            for g, z in zip(groups, zs[b, p])]
            drops = [jnp.maximum(z, 0.0) + jnp.log2(1.0 + jnp.exp2(-jnp.abs(z))) for z in zs[b, p]]
            sums[b, p] = [jnp.dot(jnp.concatenate(_bf16_terms(x, SB_SUFFIX_TERMS), axis=1), suffix_ref[...],
                                  preferred_element_type=F32) for x in drops]
            yield
        carries, accs = state["carries"], state["accs"]
        for b, p in waves:
            csums = [s + group(carries[p], g) for g, s in zip(groups, sums[b, p])]
            ws = [jnp.exp2(z - c) for z, c in zip(zs[b, p], csums)]
            if blocks[b][2] is not None:
                ws = [jnp.where(blocks[b][2], w, 0.0) for w in ws]
            vb = vb_ref[0, key_rows[b], lanes(p)].astype(BF16)
            pvs = [jnp.dot(w.astype(BF16), vb, preferred_element_type=F32) for w in ws]
            pv0 = jnp.concatenate(pvs[:n_groups // 2], axis=0)
            pv1 = jnp.concatenate(pvs[n_groups // 2:], axis=0)
            accs[p] = accs[p] + jnp.where(head0, pv0, pv1)
            carries[p] = jnp.concatenate([c[:, 0:1] for c in csums], axis=0)
            yield

    def live(carries):
        lowest = carries[0]
        for c in carries[1:]:
            lowest = jnp.minimum(lowest, c)
        return jnp.min(lowest) < SB_UNDERFLOW_LOG2

    state = {"carries": [jnp.zeros((2 * blk, 1), F32) for _ in pairs],
             "accs": [jnp.zeros((blk, PAIR), F32) for _ in pairs]}
    yield from walk([(i, True, None), (jnp.maximum(i - 1, 0), False, i > 0)], state)
    yield SB_TAIL

    def more(loop):
        return jnp.logical_and(loop[0] < i, loop[1])

    def body(loop):
        n, _, carries, accs = loop
        state = {"carries": list(carries), "accs": list(accs)}
        for _ in walk([(i - 1 - n, False, None)], state):
            pass
        return n + 1, live(state["carries"]), tuple(state["carries"]), tuple(state["accs"])

    _, _, _, accs = lax.while_loop(
        more, body, (jnp.int32(1), live(state["carries"]), tuple(state["carries"]), tuple(state["accs"])))
    for p in pairs:
        o_ref[0, :, lanes(p)] = (accs[p] * _silu(gate_refs[p][0].astype(F32))).astype(o_ref.dtype)


def _unit_lower_inverses(lows):
    c = lows[0].shape[0]
    r_i = lax.broadcasted_iota(jnp.int32, (c, c), 0)
    c_i = lax.broadcasted_iota(jnp.int32, (c, c), 1)
    eye = (r_i == c_i).astype(F32)
    same_block = (r_i // INV_BLOCK) == (c_i // INV_BLOCK)
    diags = [jnp.where(same_block, low, 0.0) for low in lows]
    offs = [jnp.where(same_block, 0.0, low) for low in lows]

    n_inv = RWKV_TERMS["inverse"]

    def neumann(ns, order):
        invs = [eye + n for n in ns]
        powers = [_mm(t, t) for t in (_bf16_terms(n, n_inv) for n in ns)]
        yield
        span = 2
        while 2 * span < order:
            stacked = [_mm(jnp.concatenate([inv, pw], axis=0), pw, na=n_inv, nb=n_inv)
                       for inv, pw in zip(invs, powers)]
            invs = [inv + st[:c] for inv, st in zip(invs, stacked)]
            powers = [st[c:] for st in stacked]
            span *= 2
            yield
        return [inv + _mm(inv, pw, na=n_inv, nb=n_inv) for inv, pw in zip(invs, powers)]

    inv_diags = yield from neumann(diags, INV_BLOCK)
    inv_diags = [_bf16_terms(inv, n_inv) for inv in inv_diags]
    yield
    remainders = [_mm(inv, off, nb=n_inv) for inv, off in zip(inv_diags, offs)]
    yield
    inv_offs = yield from neumann(remainders, c // INV_BLOCK)
    yield
    return [_mm(inv_off, inv_diag, na=n_inv) for inv_off, inv_diag in zip(inv_offs, inv_diags)]


def _rwkv_phases(r_ref, k_ref, v_ref, g_ref, lora_ref,
                 mu_r_ref, mu_k_ref, mu_v_ref, mu_g_ref, mu_l_ref,
                 w_up_ref, a_up_ref, w0_ref, a0_ref, kk_ref, ka_ref, rk_ref, gng_ref, gnb_ref,
                 o_ref, state_ref, prev_ref, *, group, lora_slot):
    c = r_ref.shape[1]
    n_pairs = len(group)
    lanes = slice(group[0] * PAIR, (group[-1] + 1) * PAIR)
    n_lora, n_sums, n_scores, n_apply = (RWKV_TERMS[s] for s in ("lora", "sums", "scores", "apply"))

    row = lax.broadcasted_iota(jnp.int32, (c, 1), 0)
    head0 = lax.broadcasted_iota(jnp.int32, (1, PAIR), 1) < HEAD_DIM
    ones_h = _head_block_ones()
    ones_h16 = ones_h.astype(BF16)
    ones_2 = _head_block_ones(2 * PAIR).astype(BF16)

    def pair(x, p):
        return x[:, p * PAIR:(p + 1) * PAIR]

    def head_sums(x):
        parts = [_mm(x[:, p * PAIR:(p + 2) * PAIR], ones_2, na=n_sums) for p in range(0, n_pairs - 1, 2)]
        if n_pairs % 2:
            parts.append(_mm(pair(x, n_pairs - 1), ones_h16, na=n_sums))
        return jnp.concatenate(parts, axis=1)

    def shifted(ref, mu_ref, slot, cols):
        x = ref[0, :, cols].astype(F32)
        prev = jnp.where(row == 0, prev_ref[slot:slot + 1, cols], pltpu.roll(x, 1, 0))
        prev_ref[slot:slot + 1, cols] = x[c - 1:c, :]
        return x + (prev - x) * mu_ref[:, cols]

    r = shifted(r_ref, mu_r_ref, 0, lanes)
    k = shifted(k_ref, mu_k_ref, 1, lanes)
    v = shifted(v_ref, mu_v_ref, 2, lanes)
    gate = shifted(g_ref, mu_g_ref, 3, lanes)
    lora = shifted(lora_ref, mu_l_ref, lora_slot, slice(0, PAIR))

    log_w = -_softplus(-(w0_ref[:, lanes]
                         + _mm(jnp.tanh(lora), w_up_ref[:, lanes], na=n_lora, nb=n_lora))) - 0.5
    log_decay = -jnp.exp(log_w)
    iclr = _sigmoid(a0_ref[:, lanes] + _mm(lora, a_up_ref[:, lanes], na=n_lora, nb=n_lora))
    kk = k * kk_ref[:, lanes]
    kk = kk / jnp.maximum(jnp.sqrt(head_sums(kk * kk)), 1e-12)
    k = k * (1.0 + (iclr - 1.0) * ka_ref[:, lanes])

    r_i = lax.broadcasted_iota(jnp.int32, (c, c), 0)
    c_i = lax.broadcasted_iota(jnp.int32, (c, c), 1)
    lower = r_i >= c_i
    strict = r_i > c_i
    lower_2 = (lax.broadcasted_iota(jnp.int32, (c, 2 * c), 0)
               >= lax.broadcasted_iota(jnp.int32, (c, 2 * c), 1) % c)
    cum = _mm(lower.astype(BF16), log_decay, nb=n_sums)
    total = cum[c - 1:c, :]
    w_incl = jnp.exp(cum)
    w_excl = jnp.exp(cum - log_decay)
    w_inv = jnp.exp(-cum)
    w_rest = jnp.exp(total - cum)
    w_all = jnp.exp(total)

    kk_a = kk * iclr
    a_all = -kk * w_excl
    r_all = r * w_incl
    b_all = kk_a * w_inv
    k_all = k * w_inv
    bw_all = kk_a * w_rest
    kw_all = k * w_rest

    pairs = range(n_pairs)
    a_ts = [pair(a_all, p) for p in pairs]
    r_ts = [pair(r_all, p) for p in pairs]
    ms = []
    for p in pairs:
        lhs = jnp.concatenate([jnp.where(head0, a_ts[p], 0.0), jnp.where(head0, 0.0, a_ts[p]),
                               jnp.where(head0, r_ts[p], 0.0), jnp.where(head0, 0.0, r_ts[p])], axis=0)
        rhs = jnp.concatenate([pair(b_all, p), pair(k_all, p)], axis=0)
        ms.append(_mm(lhs, rhs, NT, na=n_scores, nb=n_scores))
    yield
    states = [state_ref[group[p]] for p in pairs]
    state_ts = [_bf16_terms(s, n_apply) for s in states]
    v_ts = [_bf16_terms(pair(v, p), n_apply) for p in pairs]
    x0_ts = [_bf16_terms(
        _mm(a_ts[p], state_ts[p], NT, na=n_apply) + jnp.where(
            head0,
            _mm(jnp.where(strict, ms[p][0:c, c:], 0.0), v_ts[p], na=n_apply),
            _mm(jnp.where(strict, ms[p][c:2 * c, c:], 0.0), v_ts[p], na=n_apply)), n_apply)
        for p in pairs]
    yield
    invs = yield from _unit_lower_inverses([jnp.where(strict, ms[p][h * c:(h + 1) * c, :c], 0.0)
                                            for p in pairs for h in range(2)])
    yield
    us =[jnp.where(head0, _mm(invs[2 * p], x0_ts[p], na=n_apply), _mm(invs[2 * p + 1], x0_ts[p], na=n_apply))
          for p in pairs]
    yield
    uv_ts = [_bf16_terms(jnp.concatenate([us[p], pair(v, p)], axis=0), n_apply) for p in pairs]
    ys = [_mm(r_ts[p], state_ts[p], NT, na=n_apply) + jnp.where(
        head0,
        _mm(jnp.where(lower_2, ms[p][2 * c:3 * c, :], 0.0), uv_ts[p], na=n_apply),
        _mm(jnp.where(lower_2, ms[p][3 * c:, :], 0.0), uv_ts[p], na=n_apply)) for p in pairs]
    for p in pairs:
        decayed = jnp.concatenate([pair(bw_all, p), pair(kw_all, p)], axis=0)
        state_ref[group[p]] = states[p] * pair(w_all, p) + ones_h * _mm(uv_ts[p], decayed, TN, nb=n_apply)
    yield

    y = jnp.concatenate(ys, axis=1)
    mean = head_sums(y) * (1.0 / HEAD_DIM)
    yc = y - mean
    var = head_sums(yc * yc) * (1.0 / HEAD_DIM)
    yn = yc * lax.rsqrt(var + GN_EPS) * gng_ref[:, lanes] + gnb_ref[:, lanes]
    bonus = head_sums(r * k * rk_ref[:, lanes]) * v
    o_ref[0, :, lanes] = ((yn + bonus) * _silu(gate)).astype(o_ref.dtype)


N_RWKV_INPUTS = 19
N_RWKV_ACTIVATIONS = 5


def _mixers_kernel(*refs, blk):
    rwkv_in = refs[:N_RWKV_INPUTS]
    ya_ref, yc_ref, state_ref, prev_ref = refs[-4:]
    qn_ref, kn_ref, vb_ref = refs[N_RWKV_INPUTS:N_RWKV_INPUTS + 3]
    gate_refs = refs[N_RWKV_INPUTS + 3:-6]
    suffix_ref, future_ref = refs[-6:-4]
    t = pl.program_id(1)

    @pl.when(t == 0)
    def _reset():
        state_ref[...] = jnp.zeros_like(state_ref)
        prev_ref[...] = jnp.zeros_like(prev_ref)

    attentions, rwkvs = [], []
    for row in range(ya_ref.shape[0]):
        one = lambda ref, row=row: ref.at[pl.ds(row, 1)]
        attentions.append(_sb_phases(one(qn_ref), one(kn_ref), one(vb_ref), [one(g) for g in gate_refs],
                                     suffix_ref, future_ref, one(yc_ref),
                                     i=t % (kn_ref.shape[1] // blk), blk=blk))
        rwkvs.append([_rwkv_phases(*[one(r) for r in rwkv_in[:N_RWKV_ACTIVATIONS]],
                                   *rwkv_in[N_RWKV_ACTIVATIONS:], one(ya_ref),
                                   state_ref.at[row], prev_ref.at[row],
                                   group=group, lora_slot=RWKV_LORA_SLOT + n)
                      for n, group in enumerate(RWKV_GROUPS)])
    rwkvs = [chunk for same_group in zip(*rwkvs) for chunk in same_group]
    rounds = 0
    done = object()
    while [next(a) for a in attentions][0] is not SB_TAIL:
        started = rwkvs[:len(attentions) * (1 + rounds // RWKV_STAGGER)]
        for r in started:
            next(r, None)
        rounds += 1
    while rwkvs:
        rwkvs = [r for r in rwkvs if next(r, done) is not done]
    for a in attentions:
        next(a, None)


def _mixers(proj, mu_a, w_up, w0, a_up, a0, k_k, k_a, r_k, gn_g, gn_b, qn_g, kn_g):
    b, s, _ = proj.shape
    chunk, blk = RWKV_CHUNK, SB_BLOCK
    nq = s // blk
    qn, kn = _sb_prep(proj, qn_g, kn_g)

    zeros = jnp.zeros((LORA_RANK, RWKV_WIDTH), F32)
    w_up_pad = jnp.concatenate([w_up, zeros], axis=0)
    a_up_pad = jnp.concatenate([zeros, a_up], axis=0)
    mu = mu_a.reshape(1, A_COLS)
    row = lambda x: x.reshape(1, RWKV_WIDTH)

    nb = MIX_ROWS

    def act(section):
        return pl.BlockSpec((nb, chunk, RWKV_WIDTH), lambda bi, t: (bi, t, section))

    def mu_spec(section):
        return pl.BlockSpec((1, RWKV_WIDTH), lambda bi, t: (0, section))

    lora_spec = pl.BlockSpec((nb, chunk, PAIR), lambda bi, t: (bi, t, LORA_BLK))
    mu_lora_spec = pl.BlockSpec((1, PAIR), lambda bi, t: (0, LORA_BLK))
    up_spec = pl.BlockSpec((PAIR, RWKV_WIDTH), lambda bi, t: (0, 0))
    vec = pl.BlockSpec((1, RWKV_WIDTH), lambda bi, t: (0, 0))
    rwkv_specs = ([act(0), act(1), act(2), act(3), lora_spec,
                   mu_spec(0), mu_spec(1), mu_spec(2), mu_spec(3), mu_lora_spec, up_spec, up_spec] + [vec] * 7)
    rwkv_args = (proj, proj, proj, proj, proj, mu, mu, mu, mu, mu, w_up_pad, a_up_pad,
                 row(w0), row(a0), row(k_k), row(k_a), row(r_k), row(gn_g), row(gn_b))
    assert len(rwkv_specs) == len(rwkv_args) == N_RWKV_INPUTS

    width = SB_PAIRS * PAIR
    gate_blk = C_BLK + 3 * N_PAIRS
    n_terms = SB_SUFFIX_TERMS
    r_i = lax.broadcasted_iota(jnp.int32, (n_terms * blk, blk), 0) % blk
    c_i = lax.broadcasted_iota(jnp.int32, (n_terms * blk, blk), 1)
    suffix = (r_i >= c_i).astype(BF16)
    q_i = lax.broadcasted_iota(jnp.int32, (2 * blk, blk), 0) % blk
    s_i = lax.broadcasted_iota(jnp.int32, (2 * blk, blk), 1)
    future = jnp.where(s_i < q_i, 0.0, SB_MASKED).astype(F32)
    queries = pl.BlockSpec((nb, blk, width), lambda bi, t: (bi, t % nq, t // nq))
    keys = pl.BlockSpec((nb, s, width), lambda bi, t: (bi, 0, t // nq), pipeline_mode=pl.Buffered(1))
    value_blk = (C_BLK + 2 * N_PAIRS) * PAIR // width
    values = pl.BlockSpec((nb, s, width), lambda bi, t: (bi, 0, value_blk + t // nq),
                          pipeline_mode=pl.Buffered(1))
    gates = [pl.BlockSpec((nb, blk, PAIR), lambda bi, t, k=k: (bi, t % nq, gate_blk + (t // nq) * SB_PAIRS + k))
             for k in range(SB_PAIRS)]
    sb_specs = [queries, keys, values] + gates + [
        pl.BlockSpec((n_terms * blk, blk), lambda bi, t: (0, 0)),
        pl.BlockSpec((2 * blk, blk), lambda bi, t: (0, 0))]
    sb_args = (qn, kn, proj) + (proj,) * SB_PAIRS + (suffix, future)

    shape = jax.ShapeDtypeStruct((b, s, RWKV_WIDTH), BF16)
    return pl.pallas_call(
        functools.partial(_mixers_kernel, blk=blk),
        grid=(b // nb, s // chunk),
        in_specs=rwkv_specs + sb_specs,
        out_specs=[pl.BlockSpec((nb, chunk, RWKV_WIDTH), lambda bi, t: (bi, t, 0)), queries],
        out_shape=[shape, jax.ShapeDtypeStruct((b, s, SB_WIDTH), BF16)],
        scratch_shapes=[pltpu.VMEM((nb, N_PAIRS, PAIR, PAIR), F32), pltpu.VMEM((nb, 8, RWKV_WIDTH), F32)],
        compiler_params=_params(("parallel", "arbitrary")),
        name="mixers",
    )(*rwkv_args, *sb_args)


def kernel(x, norm_g, w_in, mu_a, w_up, w0, a_up, a0, k_k, k_a, r_k, gn_g, gn_b, pool_w, pool_scale,
           qn_g, kn_g, w_out):
    b, s, d = x.shape
    depth = w_in.shape[0]
    x2d = x.reshape(b * s, d)
    w_in16 = w_in.astype(BF16)
    w_out16 = w_out.astype(BF16)
    for l in range(depth):
        proj = _in_proj(x2d, norm_g[l].reshape(1, d), w_in16, l).reshape(b, s, IN_COLS)
        ya, yc = _mixers(proj, mu_a[l], w_up[l], w0[l], a_up[l], a0[l], k_k[l], k_a[l], r_k[l], gn_g[l], gn_b[l],
                         qn_g[l], kn_g[l])
        yb = _pool(proj, pool_w[l], pool_scale[l].reshape(1, POOL_WIDTH))
        x2d = _out_proj(ya.reshape(b * s, RWKV_WIDTH), yb.reshape(b * s, POOL_WIDTH),
                        yc.reshape(b * s, SB_WIDTH), w_out16, l, x2d)
    return x2d.reshape(b, s, d)
```

```python
import functools

import jax
import jax.numpy as jnp
from jax import lax
from jax.experimental import pallas as pl
from jax.experimental.pallas import tpu as pltpu

F32 = jnp.float32
BF16 = jnp.bfloat16

D_MODEL = 2048
HEAD_DIM = 64
PAIR = 2 * HEAD_DIM
RWKV_WIDTH = 768
POOL_WIDTH = 512
SB_WIDTH = 768
LORA_RANK = 64
POOL_WINDOWS = (2, 4, 8, 16)
POOL_GROUP_WIDTH = POOL_WIDTH // len(POOL_WINDOWS)
POOL_HALO = 16
RMS_EPS = 1e-6
GN_EPS = 64e-5
LOG2_E = 1.4426950408889634
A_COLS = 4 * RWKV_WIDTH + 2 * LORA_RANK
B_COLS = 2 * POOL_WIDTH
C_COLS = 4 * SB_WIDTH
IN_COLS = A_COLS + B_COLS + C_COLS
N_PAIRS = RWKV_WIDTH // PAIR

LORA_BLK = 4 * RWKV_WIDTH // PAIR
B_BLK = A_COLS // PAIR
C_BLK = (A_COLS + B_COLS) // PAIR

PROJ_DTYPE = BF16

RWKV_CHUNK = 128
INV_BLOCK = 16
RWKV_TERMS = {
    "lora": 1,
    "sums": 2,
    "scores": 1,
    "inverse": 1,
    "apply": 1,
}

SB_PAIRS = 3
SB_BLOCK = RWKV_CHUNK * (N_PAIRS // SB_PAIRS)
SB_ROW_GROUP = 256
SB_UNDERFLOW_LOG2 = 127.0
SB_MASKED = -1e30
SB_PREP_ROWS = 256
SB_NORM_TERMS = 2
SB_SUFFIX_TERMS = 1
RWKV_GROUPS = (range(0, N_PAIRS),)
RWKV_STAGGER = 4
RWKV_LORA_SLOT = 4
MIX_ROWS = 2
SB_TAIL = "tail"

VMEM_LIMIT = 56 * 1024 * 1024

NN = (((1,), (0,)), ((), ()))
NT = (((1,), (1,)), ((), ()))
TN = (((0,), (0,)), ((), ()))


def _dot16(a, b, dims=NN):
    return lax.dot_general(a.astype(BF16), b.astype(BF16), dims, preferred_element_type=F32)


def _bf16_terms(x, n):
    terms = []
    for _ in range(n - 1):
        t = x.astype(BF16)
        terms.append(t)
        x = x - t.astype(F32)
    terms.append(x.astype(BF16))
    return terms


def _mm(a, b, dims=NN, na=1, nb=1):
    a_terms = a if isinstance(a, list) else _bf16_terms(a, na)
    b_terms = b if isinstance(b, list) else _bf16_terms(b, nb)
    keep = max(len(a_terms), len(b_terms))
    out = None
    for i, at in enumerate(a_terms):
        for j, bt in enumerate(b_terms):
            if i + j < keep:
                term = lax.dot_general(at, bt, dims, preferred_element_type=F32)
                out = term if out is None else out + term
    return out


def _sigmoid(x):
    return 1.0 / (1.0 + jnp.exp(-x))


def _silu(x):
    return x * _sigmoid(x)


def _softplus(x):
    return jnp.maximum(x, 0.0) + jnp.log(1.0 + jnp.exp(-jnp.abs(x)))


def _head_block_ones(width=PAIR):
    r = lax.broadcasted_iota(jnp.int32, (width, width), 0) // HEAD_DIM
    c = lax.broadcasted_iota(jnp.int32, (width, width), 1) // HEAD_DIM
    return (r == c).astype(F32)


def _params(sem):
    return pltpu.CompilerParams(dimension_semantics=sem, vmem_limit_bytes=VMEM_LIMIT)


def _in_proj_kernel(x_ref, g_ref, w_ref, o_ref, h_ref, *, row_chunk):
    @pl.when(pl.program_id(1) == 0)
    def _norm():
        def body(c, carry):
            rows = pl.ds(pl.multiple_of(c * row_chunk, row_chunk), row_chunk)
            x = x_ref[rows, :]
            ms = jnp.mean(x * x, axis=-1, keepdims=True)
            h_ref[rows, :] = (x * lax.rsqrt(ms + RMS_EPS) * g_ref[...]).astype(BF16)
            return carry
        lax.fori_loop(0, x_ref.shape[0] // row_chunk, body, 0)

    o_ref[...] = jnp.dot(h_ref[...], w_ref[...], preferred_element_type=F32).astype(o_ref.dtype)


def _in_proj(x2d, g, w_bf16, layer, *, tm=1024, tn=2432):
    t, d = x2d.shape
    n = w_bf16.shape[2]
    return pl.pallas_call(
        functools.partial(_in_proj_kernel, row_chunk=128),
        grid=(t // tm, n // tn),
        in_specs=[
            pl.BlockSpec((tm, d), lambda i, j: (i, 0)),
            pl.BlockSpec((1, d), lambda i, j: (0, 0)),
            pl.BlockSpec((None, d, tn), lambda i, j: (layer, 0, j)),
        ],
        out_specs=pl.BlockSpec((tm, tn), lambda i, j: (i, j)),
        out_shape=jax.ShapeDtypeStruct((t, n), PROJ_DTYPE),
        scratch_shapes=[pltpu.VMEM((tm, d), BF16)],
        compiler_params=_params(("parallel", "arbitrary")),
        name="in_proj",
    )(x2d, g, w_bf16)


def _out_proj_kernel(ya_ref, yb_ref, yc_ref, w_ref, x_ref, o_ref):
    pool_row, sb_row = RWKV_WIDTH, RWKV_WIDTH + POOL_WIDTH
    acc = jnp.dot(ya_ref[...], w_ref[0:pool_row, :], preferred_element_type=F32)
    acc += jnp.dot(yb_ref[...], w_ref[pool_row:sb_row, :], preferred_element_type=F32)
    acc += jnp.dot(yc_ref[...], w_ref[sb_row:, :], preferred_element_type=F32)
    o_ref[...] = x_ref[...] + acc


def _out_proj(ya, yb, yc, w_bf16, layer, x2d, *, tm=512):
    t, d = x2d.shape
    row = lambda i: (i, 0)
    return pl.pallas_call(
        _out_proj_kernel,
        grid=(t // tm,),
        in_specs=[
            pl.BlockSpec((tm, RWKV_WIDTH), row),
            pl.BlockSpec((tm, POOL_WIDTH), row),
            pl.BlockSpec((tm, SB_WIDTH), row),
            pl.BlockSpec((None, w_bf16.shape[1], d), lambda i: (layer, 0, 0)),
            pl.BlockSpec((tm, d), row),
        ],
        out_specs=pl.BlockSpec((tm, d), row),
        out_shape=jax.ShapeDtypeStruct((t, d), F32),
        compiler_params=_params(("parallel",)),
        name="out_proj",
    )(ya, yb, yc, w_bf16, x2d)


def _pool_kernel(*refs, ts):
    n_g = len(POOL_WINDOWS)
    cur_refs = refs[0:n_g]
    halo_refs = refs[n_g:2 * n_g]
    gate_refs = refs[2 * n_g:3 * n_g]
    w_ref, scale_ref, o_ref, ext_ref = refs[3 * n_g:]
    i = pl.program_id(1)
    pos = (i * ts + 1 + lax.broadcasted_iota(jnp.int32, (ts, 1), 0)).astype(F32)
    for gi, win in enumerate(POOL_WINDOWS):
        cur = cur_refs[gi][0].astype(F32)
        ext_ref[0:POOL_HALO, :] = jnp.where(i == 0, 0.0, halo_refs[gi][0].astype(F32))
        ext_ref[POOL_HALO:, :] = cur
        acc = cur
        for lag in range(1, win):
            acc = acc + ext_ref[POOL_HALO - lag:POOL_HALO - lag + ts, :]
        d = acc / jnp.minimum(pos, float(win)) - cur
        y = _dot16(d, w_ref[gi])
        cols = slice(gi * POOL_GROUP_WIDTH, (gi + 1) * POOL_GROUP_WIDTH)
        y = y * scale_ref[:, cols] * _silu(gate_refs[gi][0].astype(F32))
        o_ref[0, :, cols] = y.astype(o_ref.dtype)


def _pool(proj, pool_w, pool_scale, *, ts=1024):
    b, s, _ = proj.shape
    ts = min(ts, s)
    n_g = len(POOL_WINDOWS)
    halo_blocks = ts // POOL_HALO
    cur_specs = [pl.BlockSpec((1, ts, POOL_GROUP_WIDTH), lambda bi, i, g=g: (bi, i, B_BLK + g))
                 for g in range(n_g)]
    halo_specs = [pl.BlockSpec((1, POOL_HALO, POOL_GROUP_WIDTH),
                               lambda bi, i, g=g: (bi, jnp.maximum(i * halo_blocks - 1, 0), B_BLK + g))
                  for g in range(n_g)]
    gate_specs = [pl.BlockSpec((1, ts, POOL_GROUP_WIDTH), lambda bi, i, g=g: (bi, i, B_BLK + n_g + g))
                  for g in range(n_g)]
    return pl.pallas_call(
        functools.partial(_pool_kernel, ts=ts),
        grid=(b, s // ts),
        in_specs=cur_specs + halo_specs + gate_specs + [
            pl.BlockSpec((n_g, POOL_GROUP_WIDTH, POOL_GROUP_WIDTH), lambda bi, i: (0, 0, 0)),
            pl.BlockSpec((1, POOL_WIDTH), lambda bi, i: (0, 0)),
        ],
        out_specs=pl.BlockSpec((1, ts, POOL_WIDTH), lambda bi, i: (bi, i, 0)),
        out_shape=jax.ShapeDtypeStruct((b, s, POOL_WIDTH), BF16),
        scratch_shapes=[pltpu.VMEM((ts + POOL_HALO, POOL_GROUP_WIDTH), F32)],
        compiler_params=_params(("parallel", "arbitrary")),
        name="pool",
    )(*([proj] * (3 * n_g)), pool_w, pool_scale)


def _sb_prep_kernel(q_ref, k_ref, qg_ref, kg_ref, qn_ref, kn_ref):
    ones_h = _head_block_ones().astype(BF16)

    def normed(x, gain):
        x = x.astype(F32)
        ms = _mm(x * x, ones_h, na=SB_NORM_TERMS) * (1.0 / HEAD_DIM)
        return x * lax.rsqrt(ms + RMS_EPS) * gain

    def body(c, carry):
        rows = pl.ds(pl.multiple_of(c * SB_PREP_ROWS, SB_PREP_ROWS), SB_PREP_ROWS)
        for p in range(q_ref.shape[2] // PAIR):
            lanes = slice(p * PAIR, (p + 1) * PAIR)
            qn_ref[0, rows, lanes] = (normed(q_ref[0, rows, lanes], qg_ref[...])
                                      * (HEAD_DIM ** -0.5 * LOG2_E)).astype(BF16)
            kn_ref[0, rows, lanes] = normed(k_ref[0, rows, lanes], kg_ref[...]).astype(BF16)
        return carry

    lax.fori_loop(0, q_ref.shape[1] // SB_PREP_ROWS, body, 0)


def _sb_prep(proj, qn_g, kn_g, *, ts=2048):
    b, s, _ = proj.shape
    ts = min(ts, s)
    width = SB_PAIRS * PAIR
    groups = SB_WIDTH // width
    first = C_BLK * PAIR // width
    qg = jnp.tile(qn_g.reshape(1, HEAD_DIM), (1, 2))
    kg = jnp.tile(kn_g.reshape(1, HEAD_DIM), (1, 2))
    vec = pl.BlockSpec((1, PAIR), lambda bi, i, j: (0, 0))
    out = pl.BlockSpec((1, ts, width), lambda bi, i, j: (bi, i, j))
    shape = jax.ShapeDtypeStruct((b, s, SB_WIDTH), BF16)
    return pl.pallas_call(
        _sb_prep_kernel,
        grid=(b, s // ts, groups),
        in_specs=[pl.BlockSpec((1, ts, width), lambda bi, i, j, sec=sec: (bi, i, first + sec * groups + j))
                  for sec in range(2)] + [vec, vec],
        out_specs=[out, out],
        out_shape=[shape, shape],
        compiler_params=_params(("parallel", "parallel", "parallel")),
        name="sb_prep",
    )(proj, proj, qg, kg)


def _sb_phases(qn_ref, kn_ref, vb_ref, gate_refs, suffix_ref, future_ref, o_ref, *, i, blk):
    pairs = range(len(gate_refs))
    head0 = lax.broadcasted_iota(jnp.int32, (1, PAIR), 1) < HEAD_DIM

    def lanes(p):
        return slice(p * PAIR, (p + 1) * PAIR)

    zero = jnp.zeros((), BF16)
    qqs = [jnp.concatenate([jnp.where(head0, qn_ref[0, :, lanes(p)], zero),
                            jnp.where(head0, zero, qn_ref[0, :, lanes(p)])], axis=0) for p in pairs]

    n_groups = 2 * blk // SB_ROW_GROUP
    groups = range(n_groups)

    def group(x, g):
        return x[g * SB_ROW_GROUP:(g + 1) * SB_ROW_GROUP]

    def walk(blocks, state):
        waves = [(b, p) for b in range(len(blocks)) for p in pairs]
        key_rows = [pl.ds(pl.multiple_of(j * blk, blk), blk) for j, _, _ in blocks]
        zs = {(b, p): [lax.dot_general(group(qqs[p], g), kn_ref[0, key_rows[b], lanes(p)], NT,
                                       preferred_element_type=F32) for g in groups] for b, p in waves}
        yield
        sums = {}
        for b, p in waves:
            if blocks[b][1]:
                zs[b, p] = [z + future_ref[g * SB_ROW_GROUP:(g + 1) * SB_ROW_GROUP, :]
                            for g, z in zip(groups, zs[b, p])]
            drops = [jnp.maximum(z, 0.0) + jnp.log2(1.0 + jnp.exp2(-jnp.abs(z))) for z in zs[b, p]]
            sums[b, p] = [jnp.dot(jnp.concatenate(_bf16_terms(x, SB_SUFFIX_TERMS), axis=1), suffix_ref[...],
                                  preferred_element_type=F32) for x in drops]
            yield
        carries, accs = state["carries"], state["accs"]
        for b, p in waves:
            carry = carries[p]
            if blocks[b][2] is not None:
                carry = carry + jnp.where(blocks[b][2], 0.0, -SB_MASKED)
            csums = [s + group(carry, g) for g, s in zip(groups, sums[b, p])]
            ws = [jnp.exp2(z - c) for z, c in zip(zs[b, p], csums)]
            vb = vb_ref[0, key_rows[b], lanes(p)].astype(BF16)
            pvs = [jnp.dot(w.astype(BF16), vb, preferred_element_type=F32) for w in ws]
            pv0 = jnp.concatenate(pvs[:n_groups // 2], axis=0)
            pv1 = jnp.concatenate(pvs[n_groups // 2:], axis=0)
            accs[p] = accs[p] + jnp.where(head0, pv0, pv1)
            carries[p] = jnp.concatenate([c[:, 0:1] for c in csums], axis=0)
            yield

    def live(carries):
        lowest = carries[0]
        for c in carries[1:]:
            lowest = jnp.minimum(lowest, c)
        return jnp.min(lowest) < SB_UNDERFLOW_LOG2

    state = {"carries": [jnp.zeros((2 * blk, 1), F32) for _ in pairs],
             "accs": [jnp.zeros((blk, PAIR), F32) for _ in pairs]}
    yield from walk([(i, True, None), (jnp.maximum(i - 1, 0), False, i > 0)], state)
    yield SB_TAIL

    def more(loop):
        return jnp.logical_and(loop[0] < i, loop[1])

    def body(loop):
        n, _, carries, accs = loop
        state = {"carries": list(carries), "accs": list(accs)}
        for _ in walk([(i - 1 - n, False, None)], state):
            pass
        return n + 1, live(state["carries"]), tuple(state["carries"]), tuple(state["accs"])

    _, _, _, accs = lax.while_loop(
        more, body, (jnp.int32(1), live(state["carries"]), tuple(state["carries"]), tuple(state["accs"])))
    for p in pairs:
        o_ref[0, :, lanes(p)] = (accs[p] * _silu(gate_refs[p][0].astype(F32))).astype(o_ref.dtype)


def _unit_lower_inverses(lows):
    c = lows[0].shape[0]
    r_i = lax.broadcasted_iota(jnp.int32, (c, c), 0)
    c_i = lax.broadcasted_iota(jnp.int32, (c, c), 1)
    eye = (r_i == c_i).astype(F32)
    same_block = (r_i // INV_BLOCK) == (c_i // INV_BLOCK)
    diags = [jnp.where(same_block, low, 0.0) for low in lows]
    offs = [jnp.where(same_block, 0.0, low) for low in lows]

    n_inv = RWKV_TERMS["inverse"]

    def neumann(ns, order):
        invs = [eye + n for n in ns]
        powers = [_mm(t, t) for t in (_bf16_terms(n, n_inv) for n in ns)]
        yield
        span = 2
        while 2 * span < order:
            stacked = [_mm(jnp.concatenate([inv, pw], axis=0), pw, na=n_inv, nb=n_inv)
                       for inv, pw in zip(invs, powers)]
            invs = [inv + st[:c] for inv, st in zip(invs, stacked)]
            powers = [st[c:] for st in stacked]
            span *= 2
            yield
        return [inv + _mm(inv, pw, na=n_inv, nb=n_inv) for inv, pw in zip(invs, powers)]

    inv_diags = yield from neumann(diags, INV_BLOCK)
    inv_diags = [_bf16_terms(inv, n_inv) for inv in inv_diags]
    yield
    remainders = [_mm(inv, off, nb=n_inv) for inv, off in zip(inv_diags, offs)]
    yield
    inv_offs = yield from neumann(remainders, c // INV_BLOCK)
    yield
    return [_mm(inv_off, inv_diag, na=n_inv) for inv_off, inv_diag in zip(inv_offs, inv_diags)]


def _rwkv_phases(r_ref, k_ref, v_ref, g_ref, lora_ref,
                 mu_r_ref, mu_k_ref, mu_v_ref, mu_g_ref, mu_l_ref,
                 w_up_ref, a_up_ref, w0_ref, a0_ref, kk_ref, ka_ref, rk_ref, gng_ref, gnb_ref,
                 o_ref, state_ref, prev_ref, *, group, lora_slot):
    c = r_ref.shape[1]
    n_pairs = len(group)
    lanes = slice(group[0] * PAIR, (group[-1] + 1) * PAIR)
    n_lora, n_sums, n_scores, n_apply = (RWKV_TERMS[s] for s in ("lora", "sums", "scores", "apply"))

    row = lax.broadcasted_iota(jnp.int32, (c, 1), 0)
    head0 = lax.broadcasted_iota(jnp.int32, (1, PAIR), 1) < HEAD_DIM
    ones_h = _head_block_ones()
    ones_h16 = ones_h.astype(BF16)
    ones_2 = _head_block_ones(2 * PAIR).astype(BF16)

    def pair(x, p):
        return x[:, p * PAIR:(p + 1) * PAIR]

    def head_sums(x):
        parts = [_mm(x[:, p * PAIR:(p + 2) * PAIR], ones_2, na=n_sums) for p in range(0, n_pairs - 1, 2)]
        if n_pairs % 2:
            parts.append(_mm(pair(x, n_pairs - 1), ones_h16, na=n_sums))
        return jnp.concatenate(parts, axis=1)

    def shifted(ref, mu_ref, slot, cols):
        x = ref[0, :, cols].astype(F32)
        prev = jnp.where(row == 0, prev_ref[slot:slot + 1, cols], pltpu.roll(x, 1, 0))
        prev_ref[slot:slot + 1, cols] = x[c - 1:c, :]
        return x + (prev - x) * mu_ref[:, cols]

    r = shifted(r_ref, mu_r_ref, 0, lanes)
    k = shifted(k_ref, mu_k_ref, 1, lanes)
    v = shifted(v_ref, mu_v_ref, 2, lanes)
    gate = shifted(g_ref, mu_g_ref, 3, lanes)
    lora = shifted(lora_ref, mu_l_ref, lora_slot, slice(0, PAIR))

    log_w = -_softplus(-(w0_ref[:, lanes]
                         + _mm(jnp.tanh(lora), w_up_ref[:, lanes], na=n_lora, nb=n_lora))) - 0.5
    log_decay = -jnp.exp(log_w)
    iclr = _sigmoid(a0_ref[:, lanes] + _mm(lora, a_up_ref[:, lanes], na=n_lora, nb=n_lora))
    kk = k * kk_ref[:, lanes]
    kk = kk / jnp.maximum(jnp.sqrt(head_sums(kk * kk)), 1e-12)
    k = k * (1.0 + (iclr - 1.0) * ka_ref[:, lanes])

    r_i = lax.broadcasted_iota(jnp.int32, (c, c), 0)
    c_i = lax.broadcasted_iota(jnp.int32, (c, c), 1)
    lower = r_i >= c_i
    strict = r_i > c_i
    lower_2 = (lax.broadcasted_iota(jnp.int32, (c, 2 * c), 0)
               >= lax.broadcasted_iota(jnp.int32, (c, 2 * c), 1) % c)
    cum = _mm(lower.astype(BF16), log_decay, nb=n_sums)
    total = cum[c - 1:c, :]
    w_incl = jnp.exp(cum)
    w_excl = jnp.exp(cum - log_decay)
    w_inv = jnp.exp(-cum)
    w_rest = jnp.exp(total - cum)
    w_all = jnp.exp(total)

    kk_a = kk * iclr
    a_all = -kk * w_excl
    r_all = r * w_incl
    b_all = kk_a * w_inv
    k_all = k * w_inv
    bw_all = kk_a * w_rest
    kw_all = k * w_rest

    pairs = range(n_pairs)
    a_ts = [pair(a_all, p) for p in pairs]
    r_ts = [pair(r_all, p) for p in pairs]
    ms = []
    for p in pairs:
        lhs = jnp.concatenate([jnp.where(head0, a_ts[p], 0.0), jnp.where(head0, 0.0, a_ts[p]),
                               jnp.where(head0, r_ts[p], 0.0), jnp.where(head0, 0.0, r_ts[p])], axis=0)
        rhs = jnp.concatenate([pair(b_all, p), pair(k_all, p)], axis=0)
        ms.append(_mm(lhs, rhs, NT, na=n_scores, nb=n_scores))
    yield
    states = [state_ref[group[p]] for p in pairs]
    state_ts = [_bf16_terms(s, n_apply) for s in states]
    v_ts = [_bf16_terms(pair(v, p), n_apply) for p in pairs]
    x0_ts = [_bf16_terms(
        _mm(a_ts[p], state_ts[p], NT, na=n_apply) + jnp.where(
            head0,
            _mm(jnp.where(strict, ms[p][0:c, c:], 0.0), v_ts[p], na=n_apply),
            _mm(jnp.where(strict, ms[p][c:2 * c, c:], 0.0), v_ts[p], na=n_apply)), n_apply)
        for p in pairs]
    yield
    invs = yield from _unit_lower_inverses([jnp.where(strict, ms[p][h * c:(h + 1) * c, :c], 0.0)
                                            for p in pairs for h in range(2)])
    yield
    us =[jnp.where(head0, _mm(invs[2 * p], x0_ts[p], na=n_apply), _mm(invs[2 * p + 1], x0_ts[p], na=n_apply))
          for p in pairs]
    yield
    uv_ts = [_bf16_terms(jnp.concatenate([us[p], pair(v, p)], axis=0), n_apply) for p in pairs]
    ys = [_mm(r_ts[p], state_ts[p], NT, na=n_apply) + jnp.where(
        head0,
        _mm(jnp.where(lower_2, ms[p][2 * c:3 * c, :], 0.0), uv_ts[p], na=n_apply),
        _mm(jnp.where(lower_2, ms[p][3 * c:, :], 0.0), uv_ts[p], na=n_apply)) for p in pairs]
    for p in pairs:
        decayed = jnp.concatenate([pair(bw_all, p), pair(kw_all, p)], axis=0)
        state_ref[group[p]] = states[p] * pair(w_all, p) + ones_h * _mm(uv_ts[p], decayed, TN, nb=n_apply)
    yield

    y = jnp.concatenate(ys, axis=1)
    mean = head_sums(y) * (1.0 / HEAD_DIM)
    yc = y - mean
    var = head_sums(yc * yc) * (1.0 / HEAD_DIM)
    yn = yc * lax.rsqrt(var + GN_EPS) * gng_ref[:, lanes] + gnb_ref[:, lanes]
    bonus = head_sums(r * k * rk_ref[:, lanes]) * v
    o_ref[0, :, lanes] = ((yn + bonus) * _silu(gate)).astype(o_ref.dtype)


N_RWKV_INPUTS = 19
N_RWKV_ACTIVATIONS = 5


def _mixers_kernel(*refs, blk):
    rwkv_in = refs[:N_RWKV_INPUTS]
    ya_ref, yc_ref, state_ref, prev_ref = refs[-4:]
    qn_ref, kn_ref, vb_ref = refs[N_RWKV_INPUTS:N_RWKV_INPUTS + 3]
    gate_refs = refs[N_RWKV_INPUTS + 3:-6]
    suffix_ref, future_ref = refs[-6:-4]
    t = pl.program_id(1)

    @pl.when(t == 0)
    def _reset():
        state_ref[...] = jnp.zeros_like(state_ref)
        prev_ref[...] = jnp.zeros_like(prev_ref)

    attentions, rwkvs = [], []
    for row in range(ya_ref.shape[0]):
        one = lambda ref, row=row: ref.at[pl.ds(row, 1)]
        attentions.append(_sb_phases(one(qn_ref), one(kn_ref), one(vb_ref), [one(g) for g in gate_refs],
                                     suffix_ref, future_ref, one(yc_ref),
                                     i=t % (kn_ref.shape[1] // blk), blk=blk))
        rwkvs.append([_rwkv_phases(*[one(r) for r in rwkv_in[:N_RWKV_ACTIVATIONS]],
                                   *rwkv_in[N_RWKV_ACTIVATIONS:], one(ya_ref),
                                   state_ref.at[row], prev_ref.at[row],
                                   group=group, lora_slot=RWKV_LORA_SLOT + n)
                      for n, group in enumerate(RWKV_GROUPS)])
    rwkvs = [chunk for same_group in zip(*rwkvs) for chunk in same_group]
    rounds = 0
    done = object()
    while [next(a) for a in attentions][0] is not SB_TAIL:
        started = rwkvs[:len(attentions) * (1 + rounds // RWKV_STAGGER)]
        for r in started:
            next(r, None)
        rounds += 1
    while rwkvs:
        rwkvs = [r for r in rwkvs if next(r, done) is not done]
    for a in attentions:
        next(a, None)


def _mixers(proj, mu_a, w_up, w0, a_up, a0, k_k, k_a, r_k, gn_g, gn_b, qn_g, kn_g):
    b, s, _ = proj.shape
    chunk, blk = RWKV_CHUNK, SB_BLOCK
    nq = s // blk
    qn, kn = _sb_prep(proj, qn_g, kn_g)

    zeros = jnp.zeros((LORA_RANK, RWKV_WIDTH), F32)
    w_up_pad = jnp.concatenate([w_up, zeros], axis=0)
    a_up_pad = jnp.concatenate([zeros, a_up], axis=0)
    mu = mu_a.reshape(1, A_COLS)
    row = lambda x: x.reshape(1, RWKV_WIDTH)

    nb = MIX_ROWS

    def act(section):
        return pl.BlockSpec((nb, chunk, RWKV_WIDTH), lambda bi, t: (bi, t, section))

    def mu_spec(section):
        return pl.BlockSpec((1, RWKV_WIDTH), lambda bi, t: (0, section))

    lora_spec = pl.BlockSpec((nb, chunk, PAIR), lambda bi, t: (bi, t, LORA_BLK))
    mu_lora_spec = pl.BlockSpec((1, PAIR), lambda bi, t: (0, LORA_BLK))
    up_spec = pl.BlockSpec((PAIR, RWKV_WIDTH), lambda bi, t: (0, 0))
    vec = pl.BlockSpec((1, RWKV_WIDTH), lambda bi, t: (0, 0))
    rwkv_specs = ([act(0), act(1), act(2), act(3), lora_spec,
                   mu_spec(0), mu_spec(1), mu_spec(2), mu_spec(3), mu_lora_spec, up_spec, up_spec] + [vec] * 7)
    rwkv_args = (proj, proj, proj, proj, proj, mu, mu, mu, mu, mu, w_up_pad, a_up_pad,
                 row(w0), row(a0), row(k_k), row(k_a), row(r_k), row(gn_g), row(gn_b))
    assert len(rwkv_specs) == len(rwkv_args) == N_RWKV_INPUTS

    width = SB_PAIRS * PAIR
    gate_blk = C_BLK + 3 * N_PAIRS
    n_terms = SB_SUFFIX_TERMS
    r_i = lax.broadcasted_iota(jnp.int32, (n_terms * blk, blk), 0) % blk
    c_i = lax.broadcasted_iota(jnp.int32, (n_terms * blk, blk), 1)
    suffix = (r_i >= c_i).astype(BF16)
    q_i = lax.broadcasted_iota(jnp.int32, (2 * blk, blk), 0) % blk
    s_i = lax.broadcasted_iota(jnp.int32, (2 * blk, blk), 1)
    future = jnp.where(s_i < q_i, 0.0, SB_MASKED).astype(F32)
    queries = pl.BlockSpec((nb, blk, width), lambda bi, t: (bi, t % nq, t // nq))
    keys = pl.BlockSpec((nb, s, width), lambda bi, t: (bi, 0, t // nq), pipeline_mode=pl.Buffered(1))
    value_blk = (C_BLK + 2 * N_PAIRS) * PAIR // width
    values = pl.BlockSpec((nb, s, width), lambda bi, t: (bi, 0, value_blk + t // nq),
                          pipeline_mode=pl.Buffered(1))
    gates = [pl.BlockSpec((nb, blk, PAIR), lambda bi, t, k=k: (bi, t % nq, gate_blk + (t // nq) * SB_PAIRS + k))
             for k in range(SB_PAIRS)]
    sb_specs = [queries, keys, values] + gates + [
        pl.BlockSpec((n_terms * blk, blk), lambda bi, t: (0, 0)),
        pl.BlockSpec((2 * blk, blk), lambda bi, t: (0, 0))]
    sb_args = (qn, kn, proj) + (proj,) * SB_PAIRS + (suffix, future)

    shape = jax.ShapeDtypeStruct((b, s, RWKV_WIDTH), BF16)
    return pl.pallas_call(
        functools.partial(_mixers_kernel, blk=blk),
        grid=(b // nb, s // chunk),
        in_specs=rwkv_specs + sb_specs,
        out_specs=[pl.BlockSpec((nb, chunk, RWKV_WIDTH), lambda bi, t: (bi, t, 0)), queries],
        out_shape=[shape, jax.ShapeDtypeStruct((b, s, SB_WIDTH), BF16)],
        scratch_shapes=[pltpu.VMEM((nb, N_PAIRS, PAIR, PAIR), F32), pltpu.VMEM((nb, 8, RWKV_WIDTH), F32)],
        compiler_params=_params(("parallel", "arbitrary")),
        name="mixers",
    )(*rwkv_args, *sb_args)


def kernel(x, norm_g, w_in, mu_a, w_up, w0, a_up, a0, k_k, k_a, r_k, gn_g, gn_b, pool_w, pool_scale,
           qn_g, kn_g, w_out):
    b, s, d = x.shape
    depth = w_in.shape[0]
    x2d = x.reshape(b * s, d)
    w_in16 = w_in.astype(BF16)
    w_out16 = w_out.astype(BF16)
    for l in range(depth):
        proj = _in_proj(x2d, norm_g[l].reshape(1, d), w_in16, l).reshape(b, s, IN_COLS)
        ya, yc = _mixers(proj, mu_a[l], w_up[l], w0[l], a_up[l], a0[l], k_k[l], k_a[l], r_k[l], gn_g[l], gn_b[l],
                         qn_g[l], kn_g[l])
        yb = _pool(proj, pool_w[l], pool_scale[l].reshape(1, POOL_WIDTH))
        x2d = _out_proj(ya.reshape(b * s, RWKV_WIDTH), yb.reshape(b * s, POOL_WIDTH),
                        yc.reshape(b * s, SB_WIDTH), w_out16, l, x2d)
    return x2d.reshape(b, s, d)
```
